```python
import jax, jax.numpy as jnp
from jax import lax
import numpy as np

D_MODEL = 1024
BATCH = 4
SEQ = 8192
DEPTH = 4
DEC_BATCH = 8
DEC_SEQ = 64
PAST_LEN = 2048

CHUNK = 64
N_A = DEPTH // 2
N_B = DEPTH - N_A
A_CHUNK = 128
A_WIDTH = D_MODEL
A_GROUPS = 8
A_GROUP_DIM = A_WIDTH // A_GROUPS
N_HEADS = 16
QK_NOPE = 64
QK_ROPE = 32
V_DIM = 64
KV_LORA = 128
Q_LORA = 256
D_FF = 2816
ROPE_BASE = 10000.0
EPS = 1e-6
Q_BLOCK = 128
ATT_SCALE = (QK_NOPE + QK_ROPE) ** -0.5

kernel_name = 'yoco_gmlp_mla_macaron_stream_step'


def rmsnorm(x, g):
    x32 = x.astype(jnp.float32)
    y = x32 * lax.rsqrt(jnp.mean(x32 * x32, axis=-1, keepdims=True) + EPS)
    return (y * g.astype(jnp.float32)).astype(x.dtype)


def swiglu(x, w_gu, w_down):
    g, u = jnp.split(x @ w_gu, 2, axis=-1)
    return (jax.nn.silu(g) * u) @ w_down


def rope(x, pos):
    half = QK_ROPE // 2
    inv = 1.0 / (ROPE_BASE ** (jnp.arange(half, dtype=jnp.float32) * (2.0 / QK_ROPE)))
    ang = pos.astype(jnp.float32)[:, None] * inv[None, :]
    shape = (ang.shape[0],) + (1,) * (x.ndim - 3) + (half,)
    cos = jnp.cos(ang).reshape(shape)
    sin = jnp.sin(ang).reshape(shape)
    x32 = x.astype(jnp.float32)
    x1, x2 = x32[..., :half], x32[..., half:]
    return jnp.concatenate([x1 * cos - x2 * sin, x2 * cos + x1 * sin], axis=-1).astype(x.dtype)


def spatial_gate(v, w_s, b_s):
    L = v.shape[-3]
    mask = jnp.tril(jnp.ones((L, L), dtype=bool))
    w = jnp.where(mask, w_s[:, :L, :L], jnp.zeros((), w_s.dtype))
    return jnp.einsum('gts,...sgd->...tgd', w, v) + b_s[:, :L].T[:, :, None]


def gmlp_mixer(h, w_in, v_g, w_s, b_s, w_out, chunked):
    z = jax.nn.gelu(h @ w_in, approximate=False)
    u, v = jnp.split(z, 2, axis=-1)
    v = rmsnorm(v, v_g)
    bsz, s, _ = v.shape
    if chunked:
        vb = v.reshape(bsz, s // A_CHUNK, A_CHUNK, A_GROUPS, A_GROUP_DIM)
    else:
        vb = v.reshape(bsz, s, A_GROUPS, A_GROUP_DIM)
    sv = spatial_gate(vb, w_s, b_s).reshape(bsz, s, A_WIDTH)
    return (u * sv) @ w_out, v


def mla_latent(h, kv_g, w_dkv, ckv_g, pos):
    kv = rmsnorm(h, kv_g) @ w_dkv
    ckv = rmsnorm(kv[..., :KV_LORA], ckv_g)
    kr = rope(kv[..., KV_LORA:], pos)
    return ckv, kr


def mla_expand(ckv, w_uk, w_uv):
    bsz, s, _ = ckv.shape
    kn = (ckv @ w_uk).reshape(bsz, s, N_HEADS, QK_NOPE)
    vv = (ckv @ w_uv).reshape(bsz, s, N_HEADS, V_DIM)
    return kn, vv


def mla_queries(h, w_dq, q_g, w_uq, pos):
    bsz, s, _ = h.shape
    q = (rmsnorm(h @ w_dq, q_g) @ w_uq).reshape(bsz, s, N_HEADS, QK_NOPE + QK_ROPE)
    return q[..., :QK_NOPE], rope(q[..., QK_NOPE:], pos)


def attend(qn, qr, kn, kr, vv, mask):
    s = jnp.einsum('bqhd,bkhd->bhqk', qn, kn) + jnp.einsum('bqhr,bkr->bhqk', qr, kr)
    s = s.astype(jnp.float32) * ATT_SCALE
    if mask is not None:
        s = jnp.where(mask[None, None], s, -jnp.inf)
    p = jax.nn.softmax(s, axis=-1).astype(vv.dtype)
    return jnp.einsum('bhqk,bkhd->bqhd', p, vv)


def mla_prompt_attention(qn, qr, kn, kr, vv):
    bsz, s = qn.shape[:2]
    nqb = s // Q_BLOCK
    qn_b = qn.reshape(bsz, nqb, Q_BLOCK, N_HEADS, QK_NOPE).transpose(1, 0, 2, 3, 4)
    qr_b = qr.reshape(bsz, nqb, Q_BLOCK, N_HEADS, QK_ROPE).transpose(1, 0, 2, 3, 4)
    k_chunk = jnp.arange(s) // CHUNK

    def block(args):
        qn_i, qr_i, i = args
        q_chunk = (i * Q_BLOCK + jnp.arange(Q_BLOCK)) // CHUNK
        mask = k_chunk[None, :] <= q_chunk[:, None]
        return attend(qn_i, qr_i, kn, kr, vv, mask)

    o = lax.map(block, (qn_b, qr_b, jnp.arange(nqb)))
    return o.transpose(1, 0, 2, 3, 4).reshape(bsz, s, N_HEADS * V_DIM)


def setup_inputs(seed: int = 0) -> dict:
    key = jax.random.key(seed)
    ks = jax.random.split(key, 32)

    def nrm(k, shape, scale):
        return jax.random.normal(k, shape, jnp.float32) * scale

    def gain(k, shape):
        return 1.0 + 0.01 * jax.random.normal(k, shape, jnp.float32)

    return {
        'x_prompt': nrm(ks[0], (BATCH, SEQ, D_MODEL), 1.0),
        'x_sample': nrm(ks[1], (DEC_BATCH, DEC_SEQ, D_MODEL), 1.0),
        'cache_ckv': nrm(ks[2], (DEC_BATCH, PAST_LEN, KV_LORA), 1.0),
        'cache_krope': nrm(ks[3], (DEC_BATCH, PAST_LEN, QK_ROPE), 1.0),
        'ffn1_norm': gain(ks[4], (DEPTH, D_MODEL)),
        'ffn1_w_gu': nrm(ks[5], (DEPTH, D_MODEL, 2 * D_FF), D_MODEL ** -0.5),
        'ffn1_w_down': nrm(ks[6], (DEPTH, D_FF, D_MODEL), D_FF ** -0.5),
        'mix_norm': gain(ks[7], (DEPTH, D_MODEL)),
        'ffn2_norm': gain(ks[8], (DEPTH, D_MODEL)),
        'ffn2_w_gu': nrm(ks[9], (DEPTH, D_MODEL, 2 * D_FF), D_MODEL ** -0.5),
        'ffn2_w_down': nrm(ks[10], (DEPTH, D_FF, D_MODEL), D_FF ** -0.5),
        'a_w_in': nrm(ks[11], (N_A, D_MODEL, 2 * A_WIDTH), D_MODEL ** -0.5),
        'a_v_norm': gain(ks[12], (N_A, A_WIDTH)),
        'a_w_s': nrm(ks[13], (N_A, A_GROUPS, A_CHUNK, A_CHUNK), A_CHUNK ** -0.5),
        'a_b_s': 1.0 + 0.02 * jax.random.normal(ks[14], (N_A, A_GROUPS, A_CHUNK), jnp.float32),
        'a_w_out': nrm(ks[15], (N_A, A_WIDTH, D_MODEL), A_WIDTH ** -0.5),
        'kv_norm': gain(ks[16], (D_MODEL,)),
        'w_dkv': nrm(ks[17], (D_MODEL, KV_LORA + QK_ROPE), D_MODEL ** -0.5),
        'ckv_norm': gain(ks[18], (KV_LORA,)),
        'w_uk': nrm(ks[19], (KV_LORA, N_HEADS * QK_NOPE), KV_LORA ** -0.5),
        'w_uv': nrm(ks[20], (KV_LORA, N_HEADS * V_DIM), KV_LORA ** -0.5),
        'b_w_dq': nrm(ks[21], (N_B, D_MODEL, Q_LORA), D_MODEL ** -0.5),
        'b_q_norm': gain(ks[22], (N_B, Q_LORA)),
        'b_w_uq': nrm(ks[23], (N_B, Q_LORA, N_HEADS * (QK_NOPE + QK_ROPE)), Q_LORA ** -0.5),
        'b_w_o': nrm(ks[24], (N_B, N_HEADS * V_DIM, D_MODEL), (N_HEADS * V_DIM) ** -0.5),
        'final_norm': gain(ks[25], (D_MODEL,)),
    }


def reference(x_prompt, x_sample, cache_ckv, cache_krope,
              ffn1_norm, ffn1_w_gu, ffn1_w_down, mix_norm, ffn2_norm, ffn2_w_gu, ffn2_w_down,
              a_w_in, a_v_norm, a_w_s, a_b_s, a_w_out,
              kv_norm, w_dkv, ckv_norm, w_uk, w_uv,
              b_w_dq, b_q_norm, b_w_uq, b_w_o, final_norm):

    def run(x, pos, is_prompt, past_ckv, past_krope):
        h = x
        a_v_rows = []
        ckv_new = kr_new = kn = kr_all = vv = None
        for l in range(DEPTH):
            h = h + 0.5 * swiglu(rmsnorm(h, ffn1_norm[l]), ffn1_w_gu[l], ffn1_w_down[l])
            hn = rmsnorm(h, mix_norm[l])
            if l < N_A:
                y, v_rows = gmlp_mixer(hn, a_w_in[l], a_v_norm[l], a_w_s[l], a_b_s[l], a_w_out[l], is_prompt)
                a_v_rows.append(v_rows)
            else:
                j = l - N_A
                qn, qr = mla_queries(hn, b_w_dq[j], b_q_norm[j], b_w_uq[j], pos)
                if is_prompt:
                    o = mla_prompt_attention(qn, qr, kn, kr_all, vv)
                else:
                    o = attend(qn, qr, kn, kr_all, vv, None).reshape(x.shape[0], x.shape[1], N_HEADS * V_DIM)
                y = o @ b_w_o[j]
            h = h + y
            h = h + 0.5 * swiglu(rmsnorm(h, ffn2_norm[l]), ffn2_w_gu[l], ffn2_w_down[l])
            if l == N_A - 1:
                ckv_new, kr_new = mla_latent(h, kv_norm, w_dkv, ckv_norm, pos)
                if is_prompt:
                    ckv_all, kr_all = ckv_new, kr_new
                else:
                    ckv_all = jnp.concatenate([past_ckv, ckv_new], axis=1)
                    kr_all = jnp.concatenate([past_krope, kr_new], axis=1)
                kn, vv = mla_expand(ckv_all, w_uk, w_uv)
        return rmsnorm(h, final_norm), ckv_new, kr_new, a_v_rows

    pos_p = jnp.arange(x_prompt.shape[1])
    y_prompt, new_ckv_prompt, new_krope_prompt, _ = run(x_prompt, pos_p, True, None, None)

    pos_s = PAST_LEN + jnp.arange(x_sample.shape[1])
    y_sample, new_ckv_sample, new_krope_sample, a_rows = run(x_sample, pos_s, False, cache_ckv, cache_krope)
    new_a_v_sample = jnp.stack(a_rows, axis=0)

    return (y_prompt, y_sample, new_ckv_prompt, new_krope_prompt, new_ckv_sample, new_krope_sample, new_a_v_sample)
```

```python
import functools
import math

import jax
import jax.numpy as jnp
from jax import lax
from jax.experimental import pallas as pl
from jax.experimental.pallas import tpu as pltpu

D_MODEL = 1024
DEPTH = 4
N_A = DEPTH // 2
CHUNK = 64
A_CHUNK = 128
A_GROUPS = 8
A_GROUP_DIM = D_MODEL // A_GROUPS
N_HEADS = 16
QK_NOPE = 64
QK_ROPE = 32
V_DIM = 64
KV_LORA = 128
Q_LORA = 256
D_FF = 2816
ROPE_BASE = 10000.0
EPS = 1e-6
ATT_SCALE = (QK_NOPE + QK_ROPE) ** -0.5
Q_SCALE = ATT_SCALE * math.log2(math.e)

LANES = 128
MXU_DIM = 256
VMEM_LIMIT = 56 * 1024 * 1024

FF_CHUNK = MXU_DIM
N_FF_CHUNKS = D_FF // FF_CHUNK
HEADS_PER_ROPE_BLOCK = LANES // QK_ROPE
QK_WIDTH = KV_LORA + LANES
V_WIDTH = 2 * KV_LORA

BF16 = jnp.bfloat16
F32 = jnp.float32


def _dot(a, b):
    return jnp.dot(a, b, preferred_element_type=F32)


def _rms(x, g):
    return x * lax.rsqrt(jnp.mean(x * x, axis=-1, keepdims=True) + EPS) * g


def _const_spec(shape):
    nd = len(shape)
    return pl.BlockSpec(shape, lambda *_: (0,) * nd, pipeline_mode=pl.Buffered(1))


def _params(n_axes=1):
    return pltpu.CompilerParams(
        dimension_semantics=("parallel",) * n_axes, vmem_limit_bytes=VMEM_LIMIT)


def _ffn_kernel(h_ref, g_ref, wgu_ref, wd_ref, *rest, final_norm):
    if final_norm:
        fg_ref, o_ref = rest
    else:
        (o_ref,) = rest
    x = h_ref[...]
    xn = _rms(x, g_ref[...]).astype(BF16)
    acc = jnp.zeros(x.shape, F32)
    for c in range(N_FF_CHUNKS):
        gu = _dot(xn, wgu_ref[:, c * 2 * FF_CHUNK:(c + 1) * 2 * FF_CHUNK])
        g = gu[:, :FF_CHUNK]
        u = gu[:, FF_CHUNK:]
        a = (g * jax.nn.sigmoid(g) * u).astype(BF16)
        acc = acc + _dot(a, wd_ref[c * FF_CHUNK:(c + 1) * FF_CHUNK, :])
    y = x + 0.5 * acc
    if final_norm:
        y = _rms(y, fg_ref[...])
    o_ref[...] = y


def _ffn(h, g, wgu, wd, final_g=None, tm=512):
    t = h.shape[0]
    tm = min(tm, t)
    tok = pl.BlockSpec((tm, D_MODEL), lambda i: (i, 0))
    in_specs = [tok, _const_spec((1, D_MODEL)), _const_spec(wgu.shape), _const_spec(wd.shape)]
    args = [h, g, wgu, wd]
    if final_g is not None:
        in_specs.append(_const_spec((1, D_MODEL)))
        args.append(final_g)
    return pl.pallas_call(
        functools.partial(_ffn_kernel, final_norm=final_g is not None),
        grid=(t // tm,),
        in_specs=in_specs,
        out_specs=tok,
        out_shape=jax.ShapeDtypeStruct((t, D_MODEL), F32),
        compiler_params=_params(),
        name="ffn",
    )(*args)


def _gmlp_kernel(h_ref, g_ref, win_ref, vg_ref, ws_ref, bias_ref, wout_ref,
                 o_ref, *rest, chunk, emit_v):
    if emit_v:
        v_ref, gate_ref = rest
    else:
        (gate_ref,) = rest
    x = h_ref[...]
    hn = _rms(x, g_ref[...]).astype(BF16)
    z = _dot(hn, win_ref[...])
    z = 0.5 * z * (1.0 + lax.erf(z * math.sqrt(0.5)))
    u = z[:, :D_MODEL]
    v = _rms(z[:, D_MODEL:], vg_ref[...])
    if emit_v:
        v_ref[...] = v
    vb = v.astype(BF16)
    row = lax.broadcasted_iota(jnp.int32, (chunk, chunk), 0)
    col = lax.broadcasted_iota(jnp.int32, (chunk, chunk), 1)
    lower = row >= col
    w = [jnp.where(lower, ws_ref[gi], 0.0).astype(BF16) for gi in range(A_GROUPS)]
    bias = bias_ref[...]
    for c in range(x.shape[0] // chunk):
        rows = slice(c * chunk, (c + 1) * chunk)
        for gi in range(A_GROUPS):
            cols = slice(gi * A_GROUP_DIM, (gi + 1) * A_GROUP_DIM)
            sv = _dot(w[gi], vb[rows, cols]) + bias[:, cols]
            gate_ref[rows, cols] = (u[rows, cols] * sv).astype(BF16)
    o_ref[...] = x + _dot(gate_ref[...], wout_ref[...])


def _gmlp(h, g, w_in, v_g, w_s, bias, w_out, chunk, emit_v, tm=256):
    t = h.shape[0]
    tm = min(tm, t)
    tok = pl.BlockSpec((tm, D_MODEL), lambda i: (i, 0))
    n_out = 2 if emit_v else 1
    return pl.pallas_call(
        functools.partial(_gmlp_kernel, chunk=chunk, emit_v=emit_v),
        grid=(t // tm,),
        in_specs=[tok, _const_spec((1, D_MODEL)), _const_spec(w_in.shape),
                  _const_spec((1, D_MODEL)), _const_spec(w_s.shape),
                  _const_spec(bias.shape), _const_spec(w_out.shape)],
        out_specs=[tok] * n_out,
        out_shape=[jax.ShapeDtypeStruct((t, D_MODEL), F32)] * n_out,
        scratch_shapes=[pltpu.VMEM((tm, D_MODEL), BF16)],
        compiler_params=_params(),
        name="gmlp",
    )(h, g, w_in, v_g, w_s, bias, w_out)


def _latent_kernel(h_ref, g_ref, wc_ref, wr_ref, wrr_ref, cg_ref, cos_ref, sin_ref,
                   ckv_ref, kr_ref):
    hn = _rms(h_ref[...], g_ref[...]).astype(BF16)
    ckv_ref[...] = _rms(_dot(hn, wc_ref[...]), cg_ref[...])
    kr_ref[...] = _dot(hn, wr_ref[...]) * cos_ref[...] + _dot(hn, wrr_ref[...]) * sin_ref[...]


def _latent(h, g, w_c, w_r, w_rr, c_g, cos, sin, tm=512):
    t = h.shape[0]
    tm = min(tm, t, cos.shape[0])
    n_pos = cos.shape[0] // tm
    tok = lambda w: pl.BlockSpec((tm, w), lambda i: (i, 0))
    pos = pl.BlockSpec((tm, QK_ROPE), lambda i: (i % n_pos, 0))
    return pl.pallas_call(
        _latent_kernel,
        grid=(t // tm,),
        in_specs=[tok(D_MODEL), _const_spec((1, D_MODEL)), _const_spec(w_c.shape),
                  _const_spec(w_r.shape), _const_spec(w_rr.shape),
                  _const_spec((1, KV_LORA)), pos, pos],
        out_specs=[tok(KV_LORA), tok(QK_ROPE)],
        out_shape=[jax.ShapeDtypeStruct((t, KV_LORA), F32),
                   jax.ShapeDtypeStruct((t, QK_ROPE), F32)],
        compiler_params=_params(),
        name="latent",
    )(h, g, w_c, w_r, w_rr, c_g, cos, sin)


def _qproj_kernel(h_ref, g_ref, wdq_ref, qg_ref, wn_ref, wk_ref, wr_ref, wrr_ref,
                  cos_ref, sin_ref, q_ref):
    hn = _rms(h_ref[...], g_ref[...]).astype(BF16)
    qc = _rms(_dot(hn, wdq_ref[...]), qg_ref[...]).astype(BF16)
    qn = _dot(qc, wn_ref[...]).astype(BF16)
    n_blocks = N_HEADS // HEADS_PER_ROPE_BLOCK
    cos = jnp.concatenate([cos_ref[...]] * n_blocks, axis=1)
    sin = jnp.concatenate([sin_ref[...]] * n_blocks, axis=1)
    qr = (_dot(qc, wr_ref[...]) * cos + _dot(qc, wrr_ref[...]) * sin) * Q_SCALE
    for p in range(N_HEADS // 2):
        qa = _dot(qn[:, p * LANES:(p + 1) * LANES], wk_ref[p]) * Q_SCALE
        q_ref[0, 2 * p, :, :KV_LORA] = qa[:, :KV_LORA].astype(BF16)
        q_ref[0, 2 * p + 1, :, :KV_LORA] = qa[:, KV_LORA:].astype(BF16)
    lane_head = lax.broadcasted_iota(jnp.int32, (qr.shape[0], LANES), 1) // QK_ROPE
    for hd in range(N_HEADS):
        blk = hd // HEADS_PER_ROPE_BLOCK
        own = lane_head == hd % HEADS_PER_ROPE_BLOCK
        q_ref[0, hd, :, KV_LORA:] = jnp.where(
            own, qr[:, blk * LANES:(blk + 1) * LANES], 0.0).astype(BF16)


def _qproj(h, g, w_dq, q_g, w_n, w_k, w_r, w_rr, cos, sin, batch, tm=256):
    t = h.shape[0]
    s = t // batch
    tm = min(tm, s)
    n_s = s // tm
    tok = pl.BlockSpec((tm, D_MODEL), lambda b, i: (b * n_s + i, 0))
    pos = pl.BlockSpec((tm, LANES), lambda b, i: (i, 0))
    return pl.pallas_call(
        _qproj_kernel,
        grid=(batch, n_s),
        in_specs=[tok, _const_spec((1, D_MODEL)), _const_spec(w_dq.shape),
                  _const_spec((1, Q_LORA)), _const_spec(w_n.shape), _const_spec(w_k.shape),
                  _const_spec(w_r.shape), _const_spec(w_rr.shape), pos, pos],
        out_specs=pl.BlockSpec((1, N_HEADS, tm, QK_WIDTH), lambda b, i: (b, 0, i, 0)),
        out_shape=jax.ShapeDtypeStruct((batch, N_HEADS, s, QK_WIDTH), BF16),
        compiler_params=_params(2),
        name="qproj",
    )(h, g, w_dq, q_g, w_n, w_k, w_r, w_rr, cos, sin)


def _attn_kernel(q_ref, k_ref, v_ref, o_ref, m_ref, acc_ref, *, tq, tk, causal, n_keys):
    rows = N_HEADS * tq
    q = q_ref[0].reshape(rows, QK_WIDTH)
    m_ref[...] = jnp.full(m_ref.shape, -jnp.inf, F32)
    acc_ref[...] = jnp.zeros(acc_ref.shape, F32)

    if causal:
        q_start = pl.program_id(1) * tq
        n_full = (q_start + CHUNK) // tk
        n_total = (q_start + tq - 1) // tk + 1
        q_pos = q_start + lax.broadcasted_iota(jnp.int32, (rows, 1), 0) % tq
        limit = (q_pos // CHUNK + 1) * CHUNK
    else:
        n_full = n_keys // tk
        n_total = k_ref.shape[1]
        limit = n_keys

    def step(kt, masked):
        s = _dot(q, k_ref[0, kt])
        if masked:
            k_pos = kt * tk + lax.broadcasted_iota(jnp.int32, (1, tk), 1)
            s = jnp.where(k_pos < limit, s, -jnp.inf)
        m_prev = m_ref[...]
        m_new = jnp.maximum(m_prev, jnp.max(s, axis=1, keepdims=True))
        p = jnp.exp2(s - m_new[:, :1]).astype(BF16)
        alpha = jnp.exp2(m_prev - m_new)
        alpha = jnp.concatenate([alpha] * (V_WIDTH // LANES), axis=1)
        acc_ref[...] = acc_ref[...] * alpha + _dot(p, v_ref[0, kt])
        m_ref[...] = m_new

    def full_body(kt, carry):
        step(kt, False)
        return carry

    def masked_body(kt, carry):
        step(kt, True)
        return carry

    lax.fori_loop(0, n_full, full_body, 0)
    lax.fori_loop(n_full, n_total, masked_body, 0)

    acc = acc_ref[...]
    o = (acc[:, :KV_LORA] / acc[:, KV_LORA:]).astype(BF16)
    for hd in range(N_HEADS):
        o_ref[0, :, hd * KV_LORA:(hd + 1) * KV_LORA] = o[hd * tq:(hd + 1) * tq]


def _attention(q, k_t, v_aug, causal, n_keys, tq=128):
    batch, _, s, _ = q.shape
    _, n_kt, _, tk = k_t.shape
    tq = min(tq, s)
    rows = N_HEADS * tq
    return pl.pallas_call(
        functools.partial(_attn_kernel, tq=tq, tk=tk, causal=causal, n_keys=n_keys),
        grid=(batch, s // tq),
        in_specs=[pl.BlockSpec((1, N_HEADS, tq, QK_WIDTH), lambda b, i: (b, 0, i, 0)),
                  pl.BlockSpec((1, n_kt, QK_WIDTH, tk), lambda b, i: (b, 0, 0, 0)),
                  pl.BlockSpec((1, n_kt, tk, V_WIDTH), lambda b, i: (b, 0, 0, 0))],
        out_specs=pl.BlockSpec((1, tq, N_HEADS * KV_LORA), lambda b, i: (b, i, 0)),
        out_shape=jax.ShapeDtypeStruct((batch, s, N_HEADS * KV_LORA), BF16),
        scratch_shapes=[pltpu.VMEM((rows, LANES), F32), pltpu.VMEM((rows, V_WIDTH), F32)],
        compiler_params=_params(2),
        name="attention",
    )(q, k_t, v_aug)


def _oproj_kernel(h_ref, a_ref, wv_ref, wo_ref, o_ref, cat_ref):
    for p in range(N_HEADS // 2):
        cat_ref[:, p * LANES:(p + 1) * LANES] = _dot(
            a_ref[:, p * 2 * KV_LORA:(p + 1) * 2 * KV_LORA], wv_ref[p]).astype(BF16)
    o_ref[...] = h_ref[...] + _dot(cat_ref[...], wo_ref[...])


def _oproj(h, a, w_v, w_o, tm=512):
    t = h.shape[0]
    tm = min(tm, t)
    tok = lambda w: pl.BlockSpec((tm, w), lambda i: (i, 0))
    return pl.pallas_call(
        _oproj_kernel,
        grid=(t // tm,),
        in_specs=[tok(D_MODEL), tok(N_HEADS * KV_LORA), _const_spec(w_v.shape),
                  _const_spec(w_o.shape)],
        out_specs=tok(D_MODEL),
        out_shape=jax.ShapeDtypeStruct((t, D_MODEL), F32),
        scratch_shapes=[pltpu.VMEM((tm, D_MODEL), BF16)],
        compiler_params=_params(),
        name="oproj",
    )(h, a, w_v, w_o)


def _block_diag_pairs(w):
    n2, r, c = w.shape
    w = w.reshape(n2 // 2, 2, r, c)
    z = jnp.zeros_like(w[:, 0])
    top = jnp.concatenate([w[:, 0], z], axis=2)
    bot = jnp.concatenate([z, w[:, 1]], axis=2)
    return jnp.concatenate([top, bot], axis=1)


def _rot_half_cols(w):
    shp = w.shape
    w = w.reshape(shp[0], -1, 2, QK_ROPE // 2)
    return jnp.stack([-w[:, :, 1], w[:, :, 0]], axis=2).reshape(shp)


def _rope_tables(pos):
    half = QK_ROPE // 2
    inv = 1.0 / (ROPE_BASE ** (jnp.arange(half, dtype=F32) * (2.0 / QK_ROPE)))
    ang = pos.astype(F32)[:, None] * inv[None, :]
    cos = jnp.cos(ang)
    sin = jnp.sin(ang)
    return jnp.concatenate([cos, cos], axis=1), jnp.concatenate([sin, sin], axis=1)


def _key_value_tiles(ckv, kr, tk):
    b, k, _ = ckv.shape
    n = -(-k // tk)
    pad = n * tk - k
    ckv = ckv.astype(BF16)
    keys = jnp.concatenate([ckv] + [kr.astype(BF16)] * HEADS_PER_ROPE_BLOCK, axis=2)
    vals = jnp.concatenate([ckv, jnp.ones_like(ckv)], axis=2)
    keys = jnp.pad(keys, ((0, 0), (0, pad), (0, 0)))
    vals = jnp.pad(vals, ((0, 0), (0, pad), (0, 0)))
    keys = keys.reshape(b, n, tk, QK_WIDTH).transpose(0, 1, 3, 2)
    return keys, vals.reshape(b, n, tk, V_WIDTH)


def kernel(x_prompt, x_sample, cache_ckv, cache_krope, ffn1_norm, ffn1_w_gu, ffn1_w_down, mix_norm, ffn2_norm, ffn2_w_gu, ffn2_w_down, a_w_in, a_v_norm, a_w_s, a_b_s, a_w_out, kv_norm, w_dkv, ckv_norm, w_uk, w_uv, b_w_dq, b_q_norm, b_w_uq, b_w_o, final_norm):
    n_b = DEPTH - N_A
    row = lambda v: v.reshape(1, -1)

    def ffn_weights(w_gu, w_down):
        g = w_gu[:, :D_FF].reshape(D_MODEL, N_FF_CHUNKS, FF_CHUNK)
        u = w_gu[:, D_FF:].reshape(D_MODEL, N_FF_CHUNKS, FF_CHUNK)
        gu = jnp.concatenate([g, u], axis=2).reshape(D_MODEL, 2 * D_FF)
        return gu.astype(BF16), w_down.astype(BF16)

    ffn1 = [ffn_weights(ffn1_w_gu[l], ffn1_w_down[l]) for l in range(DEPTH)]
    ffn2 = [ffn_weights(ffn2_w_gu[l], ffn2_w_down[l]) for l in range(DEPTH)]
    a_in = a_w_in.astype(BF16)
    a_out = a_w_out.astype(BF16)

    w_c = w_dkv[:, :KV_LORA].astype(BF16)
    w_kr = w_dkv[:, KV_LORA:]
    w_kr_rot = _rot_half_cols(w_kr).astype(BF16)
    w_kr = w_kr.astype(BF16)

    w_uq = b_w_uq.reshape(n_b, Q_LORA, N_HEADS, QK_NOPE + QK_ROPE)
    w_qn = w_uq[..., :QK_NOPE].reshape(n_b, Q_LORA, N_HEADS * QK_NOPE).astype(BF16)
    w_qr = w_uq[..., QK_NOPE:].reshape(n_b, Q_LORA, N_HEADS * QK_ROPE)
    w_qr_rot = jnp.stack([_rot_half_cols(w_qr[j]) for j in range(n_b)]).astype(BF16)
    w_qr = w_qr.astype(BF16)
    w_dq = b_w_dq.astype(BF16)
    uk_heads = w_uk.reshape(KV_LORA, N_HEADS, QK_NOPE).transpose(1, 2, 0)
    w_absorb = _block_diag_pairs(uk_heads).astype(BF16)
    uv_heads = w_uv.reshape(KV_LORA, N_HEADS, V_DIM).transpose(1, 0, 2)
    w_expand = _block_diag_pairs(uv_heads).astype(BF16)
    w_o = b_w_o.astype(BF16)

    def run(x, pos, chunk, past_ckv, past_krope, tk):
        batch, s, _ = x.shape
        causal = past_ckv is None
        h = x.reshape(batch * s, D_MODEL)
        cos, sin = _rope_tables(pos)
        cos_q = jnp.concatenate([cos] * HEADS_PER_ROPE_BLOCK, axis=1)
        sin_q = jnp.concatenate([sin] * HEADS_PER_ROPE_BLOCK, axis=1)
        v_rows = []
        ckv = kr = k_t = v_aug = None
        n_keys = s
        for l in range(DEPTH):
            h = _ffn(h, row(ffn1_norm[l]), *ffn1[l])
            if l < N_A:
                bias = jnp.repeat(a_b_s[l][:, :chunk].T, A_GROUP_DIM, axis=1)
                h, *v = _gmlp(h, row(mix_norm[l]), a_in[l], row(a_v_norm[l]),
                              a_w_s[l][:, :chunk, :chunk], bias, a_out[l], chunk,
                              emit_v=not causal)
                v_rows.extend(vi.reshape(batch, s, D_MODEL) for vi in v)
            else:
                j = l - N_A
                q = _qproj(h, row(mix_norm[l]), w_dq[j], row(b_q_norm[j]), w_qn[j], w_absorb,
                           w_qr[j], w_qr_rot[j], cos_q, sin_q, batch)
                a = _attention(q, k_t, v_aug, causal, n_keys)
                h = _oproj(h, a.reshape(batch * s, N_HEADS * KV_LORA), w_expand, w_o[j])
            h = _ffn(h, row(ffn2_norm[l]), *ffn2[l],
                     final_g=row(final_norm) if l == DEPTH - 1 else None)
            if l == N_A - 1:
                ckv, kr = _latent(h, row(kv_norm), w_c, w_kr, w_kr_rot, row(ckv_norm), cos, sin)
                ckv = ckv.reshape(batch, s, KV_LORA)
                kr = kr.reshape(batch, s, QK_ROPE)
                if causal:
                    ckv_all, kr_all = ckv, kr
                else:
                    ckv_all = jnp.concatenate([past_ckv, ckv], axis=1)
                    kr_all = jnp.concatenate([past_krope, kr], axis=1)
                n_keys = ckv_all.shape[1]
                k_t, v_aug = _key_value_tiles(ckv_all, kr_all, tk)
        return h.reshape(batch, s, D_MODEL), ckv, kr, v_rows

    s_p = x_prompt.shape[1]
    y_p, ckv_p, kr_p, _ = run(x_prompt, jnp.arange(s_p), A_CHUNK, None, None, MXU_DIM)
    past = cache_ckv.shape[1]
    s_s = x_sample.shape[1]
    y_s, ckv_s, kr_s, a_rows = run(x_sample, past + jnp.arange(s_s), s_s, cache_ckv, cache_krope, MXU_DIM)
    return (y_p, y_s, ckv_p, kr_p, ckv_s, kr_s, jnp.stack(a_rows, axis=0))
```

```python
import functools
import math

import jax
import jax.numpy as jnp
from jax import lax
from jax.experimental import pallas as pl
from jax.experimental.pallas import tpu as pltpu

D_MODEL = 1024
DEPTH = 4
N_A = DEPTH // 2
CHUNK = 64
A_CHUNK = 128
A_GROUPS = 8
A_GROUP_DIM = D_MODEL // A_GROUPS
N_HEADS = 16
QK_NOPE = 64
QK_ROPE = 32
V_DIM = 64
KV_LORA = 128
Q_LORA = 256
D_FF = 2816
ROPE_BASE = 10000.0
EPS = 1e-6
ATT_SCALE = (QK_NOPE + QK_ROPE) ** -0.5
Q_SCALE = ATT_SCALE * math.log2(math.e)

LANES = 128
MXU_DIM = 256
VMEM_LIMIT = 56 * 1024 * 1024

FF_CHUNK = MXU_DIM
N_FF_CHUNKS = D_FF // FF_CHUNK
HEADS_PER_ROPE_BLOCK = LANES // QK_ROPE
QK_WIDTH = KV_LORA + LANES
ATT_TQ = LANES
ATT_TK = 2 * MXU_DIM
ATT_COLS = MXU_DIM

BF16 = jnp.bfloat16
F32 = jnp.float32


def _dot(a, b):
    return jnp.dot(a, b, preferred_element_type=F32)


def _rms(x, g):
    return x * lax.rsqrt(jnp.mean(x * x, axis=-1, keepdims=True) + EPS) * g


def _const_spec(shape):
    nd = len(shape)
    return pl.BlockSpec(shape, lambda *_: (0,) * nd, pipeline_mode=pl.Buffered(1))


def _params(n_axes=1):
    return pltpu.CompilerParams(
        dimension_semantics=("parallel",) * n_axes, vmem_limit_bytes=VMEM_LIMIT)


def _ffn_kernel(h_ref, g_ref, wgu_ref, wd_ref, *rest, final_norm):
    if final_norm:
        fg_ref, o_ref = rest
    else:
        (o_ref,) = rest
    x = h_ref[...]
    xn = _rms(x, g_ref[...]).astype(BF16)
    acc = jnp.zeros(x.shape, F32)
    for c in range(N_FF_CHUNKS):
        gu = _dot(xn, wgu_ref[:, c * 2 * FF_CHUNK:(c + 1) * 2 * FF_CHUNK])
        g = gu[:, :FF_CHUNK]
        u = gu[:, FF_CHUNK:]
        a = (g * jax.nn.sigmoid(g) * u).astype(BF16)
        acc = acc + _dot(a, wd_ref[c * FF_CHUNK:(c + 1) * FF_CHUNK, :])
    y = x + 0.5 * acc
    if final_norm:
        y = _rms(y, fg_ref[...])
    o_ref[...] = y


def _ffn(h, g, wgu, wd, final_g=None, tm=512):
    t = h.shape[0]
    tm = min(tm, t)
    tok = pl.BlockSpec((tm, D_MODEL), lambda i: (i, 0))
    in_specs = [tok, _const_spec((1, D_MODEL)), _const_spec(wgu.shape), _const_spec(wd.shape)]
    args = [h, g, wgu, wd]
    if final_g is not None:
        in_specs.append(_const_spec((1, D_MODEL)))
        args.append(final_g)
    return pl.pallas_call(
        functools.partial(_ffn_kernel, final_norm=final_g is not None),
        grid=(t // tm,),
        in_specs=in_specs,
        out_specs=tok,
        out_shape=jax.ShapeDtypeStruct((t, D_MODEL), F32),
        compiler_params=_params(),
        name="ffn",
    )(*args)


def _gmlp_kernel(h_ref, g_ref, win_ref, vg_ref, ws_ref, bias_ref, wout_ref,
                 o_ref, *rest, chunk, emit_v):
    if emit_v:
        v_ref, gate_ref = rest
    else:
        (gate_ref,) = rest
    x = h_ref[...]
    hn = _rms(x, g_ref[...]).astype(BF16)
    z = _dot(hn, win_ref[...])
    z = 0.5 * z * (1.0 + lax.erf(z * math.sqrt(0.5)))
    u = z[:, :D_MODEL]
    v = _rms(z[:, D_MODEL:], vg_ref[...])
    if emit_v:
        v_ref[...] = v
    vb = v.astype(BF16)
    row = lax.broadcasted_iota(jnp.int32, (chunk, chunk), 0)
    col = lax.broadcasted_iota(jnp.int32, (chunk, chunk), 1)
    lower = row >= col
    w = [jnp.where(lower, ws_ref[gi], 0.0).astype(BF16) for gi in range(A_GROUPS)]
    bias = bias_ref[...]
    for c in range(x.shape[0] // chunk):
        rows = slice(c * chunk, (c + 1) * chunk)
        for gi in range(A_GROUPS):
            cols = slice(gi * A_GROUP_DIM, (gi + 1) * A_GROUP_DIM)
            sv = _dot(w[gi], vb[rows, cols]) + bias[:, cols]
            gate_ref[rows, cols] = (u[rows, cols] * sv).astype(BF16)
    o_ref[...] = x + _dot(gate_ref[...], wout_ref[...])


def _gmlp(h, g, w_in, v_g, w_s, bias, w_out, chunk, emit_v, tm=256):
    t = h.shape[0]
    tm = min(tm, t)
    tok = pl.BlockSpec((tm, D_MODEL), lambda i: (i, 0))
    n_out = 2 if emit_v else 1
    return pl.pallas_call(
        functools.partial(_gmlp_kernel, chunk=chunk, emit_v=emit_v),
        grid=(t // tm,),
        in_specs=[tok, _const_spec((1, D_MODEL)), _const_spec(w_in.shape),
                  _const_spec((1, D_MODEL)), _const_spec(w_s.shape),
                  _const_spec(bias.shape), _const_spec(w_out.shape)],
        out_specs=[tok] * n_out,
        out_shape=[jax.ShapeDtypeStruct((t, D_MODEL), F32)] * n_out,
        scratch_shapes=[pltpu.VMEM((tm, D_MODEL), BF16)],
        compiler_params=_params(),
        name="gmlp",
    )(h, g, w_in, v_g, w_s, bias, w_out)


def _latent_kernel(h_ref, g_ref, wc_ref, wr_ref, wrr_ref, cg_ref, cos_ref, sin_ref,
                   ckv_ref, kr_ref):
    hn = _rms(h_ref[...], g_ref[...]).astype(BF16)
    ckv_ref[...] = _rms(_dot(hn, wc_ref[...]), cg_ref[...])
    kr_ref[...] = _dot(hn, wr_ref[...]) * cos_ref[...] + _dot(hn, wrr_ref[...]) * sin_ref[...]


def _latent(h, g, w_c, w_r, w_rr, c_g, cos, sin, tm=512):
    t = h.shape[0]
    tm = min(tm, t, cos.shape[0])
    n_pos = cos.shape[0] // tm
    tok = lambda w: pl.BlockSpec((tm, w), lambda i: (i, 0))
    pos = pl.BlockSpec((tm, QK_ROPE), lambda i: (i % n_pos, 0))
    return pl.pallas_call(
        _latent_kernel,
        grid=(t // tm,),
        in_specs=[tok(D_MODEL), _const_spec((1, D_MODEL)), _const_spec(w_c.shape),
                  _const_spec(w_r.shape), _const_spec(w_rr.shape),
                  _const_spec((1, KV_LORA)), pos, pos],
        out_specs=[tok(KV_LORA), tok(QK_ROPE)],
        out_shape=[jax.ShapeDtypeStruct((t, KV_LORA), F32),
                   jax.ShapeDtypeStruct((t, QK_ROPE), F32)],
        compiler_params=_params(),
        name="latent",
    )(h, g, w_c, w_r, w_rr, c_g, cos, sin)


def _qproj_kernel(h_ref, g_ref, wdq_ref, qg_ref, wn_ref, wk_ref, wr_ref, wrr_ref,
                  cos_ref, sin_ref, qt_ref):
    tm = h_ref.shape[0]
    cols = cos_ref.shape[1]
    hn = _rms(h_ref[...], g_ref[...]).astype(BF16)
    qc = _rms(_dot(hn, wdq_ref[...]), qg_ref[...])
    if tm < cols:
        qc = jnp.concatenate([qc, jnp.zeros((cols - tm, Q_LORA), F32)], axis=0)
    qc_t = qc.T.astype(BF16)
    qn_t = _dot(wn_ref[...], qc_t).astype(BF16)
    n_blocks = N_HEADS // HEADS_PER_ROPE_BLOCK
    cos = jnp.concatenate([cos_ref[...]] * n_blocks, axis=0)
    sin = jnp.concatenate([sin_ref[...]] * n_blocks, axis=0)
    qr_t = (_dot(wr_ref[...], qc_t) * cos + _dot(wrr_ref[...], qc_t) * sin) * Q_SCALE
    n_tiles = cols // ATT_TQ

    def put(hd, d0, x):
        for i in range(n_tiles):
            qt_ref[0, i, d0:d0 + LANES, hd * ATT_TQ:(hd + 1) * ATT_TQ] = (
                x[:, i * ATT_TQ:(i + 1) * ATT_TQ].astype(BF16))

    for p in range(N_HEADS // 2):
        qa_t = _dot(wk_ref[p], qn_t[p * LANES:(p + 1) * LANES]) * Q_SCALE
        put(2 * p, 0, qa_t[:KV_LORA])
        put(2 * p + 1, 0, qa_t[KV_LORA:])
    row_head = lax.broadcasted_iota(jnp.int32, (LANES, cols), 0) // QK_ROPE
    for hd in range(N_HEADS):
        blk = hd // HEADS_PER_ROPE_BLOCK
        own = row_head == hd % HEADS_PER_ROPE_BLOCK
        put(hd, KV_LORA, jnp.where(own, qr_t[blk * LANES:(blk + 1) * LANES], 0.0))


def _qproj(h, g, w_dq, q_g, w_n, w_k, w_r, w_rr, cos_t, sin_t, batch, tm=256):
    t = h.shape[0]
    s = t // batch
    tm = min(tm, s)
    n_s = s // tm
    cols = max(tm, ATT_TQ)
    n_tiles = cols // ATT_TQ
    tok = pl.BlockSpec((tm, D_MODEL), lambda b, i: (b * n_s + i, 0))
    pos = pl.BlockSpec((LANES, cols), lambda b, i: (0, i))
    return pl.pallas_call(
        _qproj_kernel,
        grid=(batch, n_s),
        in_specs=[tok, _const_spec((1, D_MODEL)), _const_spec(w_dq.shape),
                  _const_spec((1, Q_LORA)), _const_spec(w_n.shape), _const_spec(w_k.shape),
                  _const_spec(w_r.shape), _const_spec(w_rr.shape), pos, pos],
        out_specs=pl.BlockSpec((1, n_tiles, QK_WIDTH, N_HEADS * ATT_TQ),
                               lambda b, i: (b, i, 0, 0)),
        out_shape=jax.ShapeDtypeStruct(
            (batch, n_s * n_tiles, QK_WIDTH, N_HEADS * ATT_TQ), BF16),
        compiler_params=_params(2),
        name="qproj",
    )(h, g, w_dq, q_g, w_n, w_k, w_r, w_rr, cos_t, sin_t)


def _attn_kernel(qt_ref, k_ref, vt_ref, o_ref, s0_ref, s1_ref, m_ref, l_ref, acc_ref, *,
                 causal, n_keys):
    n_cols = N_HEADS * ATT_TQ
    m_ref[...] = jnp.full(m_ref.shape, -jnp.inf, F32)
    l_ref[...] = jnp.zeros(l_ref.shape, F32)
    acc_ref[...] = jnp.zeros(acc_ref.shape, F32)

    if causal:
        q_start = pl.program_id(1) * ATT_TQ
        n_full = q_start // ATT_TK
        q_pos = q_start + lax.broadcasted_iota(jnp.int32, (1, n_cols), 1) % ATT_TQ
        limit = (q_pos // CHUNK + 1) * CHUNK
    else:
        n_full = (n_keys - 1) // ATT_TK
        limit = n_keys

    s_refs = (s0_ref, s1_ref)

    def scores(kt, slot):
        s_refs[slot][...] = _dot(k_ref[0, kt], qt_ref[0, 0])

    def softmax_pv(kt, slot, masked):
        s = s_refs[slot][...]
        if masked:
            k_pos = kt * ATT_TK + lax.broadcasted_iota(jnp.int32, (ATT_TK, 1), 0)
            s = jnp.where(k_pos < limit, s, -jnp.inf)
        m_prev = m_ref[...]
        m_new = jnp.maximum(m_prev, jnp.max(s, axis=0, keepdims=True))
        p = jnp.exp2(s - m_new)
        alpha = jnp.exp2(m_prev - m_new)
        l_ref[...] = alpha * l_ref[...] + jnp.sum(p, axis=0, keepdims=True)
        acc_ref[...] = acc_ref[...] * alpha + _dot(vt_ref[0, kt], p.astype(BF16))
        m_ref[...] = m_new

    def pair_body(i, carry):
        kt = 2 * i
        scores(kt + 1, 1)
        softmax_pv(kt, 0, False)
        scores(kt + 2, 0)
        softmax_pv(kt + 1, 1, False)
        return carry

    scores(0, 0)
    lax.fori_loop(0, n_full // 2, pair_body, 0)
    if causal:
        odd = n_full % 2 == 1

        @pl.when(odd)
        def _():
            scores(n_full, 1)
            softmax_pv(n_full - 1, 0, False)
            softmax_pv(n_full, 1, True)

        @pl.when(jnp.logical_not(odd))
        def _():
            softmax_pv(n_full, 0, True)
    elif n_full % 2 == 1:
        scores(n_full, 1)
        softmax_pv(n_full - 1, 0, False)
        softmax_pv(n_full, 1, True)
    else:
        softmax_pv(n_full, 0, True)

    o = acc_ref[...] / l_ref[...]
    rows_out = o_ref.shape[1]
    for hd in range(N_HEADS):
        o_hd = o[:, hd * ATT_TQ:(hd + 1) * ATT_TQ].T
        o_ref[0, :, hd * KV_LORA:(hd + 1) * KV_LORA] = o_hd[:rows_out].astype(BF16)


def _attention(qt, keys, vals_t, s, causal, n_keys):
    batch, n_q = qt.shape[:2]
    n_kt = keys.shape[1]
    rows_out = min(s, ATT_TQ)
    n_cols = N_HEADS * ATT_TQ
    return pl.pallas_call(
        functools.partial(_attn_kernel, causal=causal, n_keys=n_keys),
        grid=(batch, n_q),
        in_specs=[pl.BlockSpec((1, 1, QK_WIDTH, n_cols), lambda b, i: (b, i, 0, 0)),
                  pl.BlockSpec((1, n_kt, ATT_TK, QK_WIDTH), lambda b, i: (b, 0, 0, 0)),
                  pl.BlockSpec((1, n_kt, KV_LORA, ATT_TK), lambda b, i: (b, 0, 0, 0))],
        out_specs=pl.BlockSpec((1, rows_out, N_HEADS * KV_LORA), lambda b, i: (b, i, 0)),
        out_shape=jax.ShapeDtypeStruct((batch, s, N_HEADS * KV_LORA), BF16),
        scratch_shapes=[pltpu.VMEM((ATT_TK, n_cols), F32), pltpu.VMEM((ATT_TK, n_cols), F32),
                        pltpu.VMEM((1, n_cols), F32), pltpu.VMEM((1, n_cols), F32),
                        pltpu.VMEM((KV_LORA, n_cols), F32)],
        compiler_params=_params(2),
        name="attention",
    )(qt, keys, vals_t)


def _oproj_kernel(h_ref, a_ref, wv_ref, wo_ref, o_ref, cat_ref):
    for p in range(N_HEADS // 2):
        cat_ref[:, p * LANES:(p + 1) * LANES] = _dot(
            a_ref[:, p * 2 * KV_LORA:(p + 1) * 2 * KV_LORA], wv_ref[p]).astype(BF16)
    o_ref[...] = h_ref[...] + _dot(cat_ref[...], wo_ref[...])


def _oproj(h, a, w_v, w_o, tm=512):
    t = h.shape[0]
    tm = min(tm, t)
    tok = lambda w: pl.BlockSpec((tm, w), lambda i: (i, 0))
    return pl.pallas_call(
        _oproj_kernel,
        grid=(t // tm,),
        in_specs=[tok(D_MODEL), tok(N_HEADS * KV_LORA), _const_spec(w_v.shape),
                  _const_spec(w_o.shape)],
        out_specs=tok(D_MODEL),
        out_shape=jax.ShapeDtypeStruct((t, D_MODEL), F32),
        scratch_shapes=[pltpu.VMEM((tm, D_MODEL), BF16)],
        compiler_params=_params(),
        name="oproj",
    )(h, a, w_v, w_o)


def _block_diag_pairs(w):
    n2, r, c = w.shape
    w = w.reshape(n2 // 2, 2, r, c)
    z = jnp.zeros_like(w[:, 0])
    top = jnp.concatenate([w[:, 0], z], axis=2)
    bot = jnp.concatenate([z, w[:, 1]], axis=2)
    return jnp.concatenate([top, bot], axis=1)


def _rot_half_cols(w):
    shp = w.shape
    w = w.reshape(shp[0], -1, 2, QK_ROPE // 2)
    return jnp.stack([-w[:, :, 1], w[:, :, 0]], axis=2).reshape(shp)


def _rope_tables(pos):
    half = QK_ROPE // 2
    inv = 1.0 / (ROPE_BASE ** (jnp.arange(half, dtype=F32) * (2.0 / QK_ROPE)))
    ang = pos.astype(F32)[:, None] * inv[None, :]
    cos = jnp.cos(ang)
    sin = jnp.sin(ang)
    return jnp.concatenate([cos, cos], axis=1), jnp.concatenate([sin, sin], axis=1)


def _key_value_tiles(ckv, kr):
    b, k, _ = ckv.shape
    n = -(-k // ATT_TK)
    pad = ((0, 0), (0, n * ATT_TK - k), (0, 0))
    ckv = jnp.pad(ckv.astype(BF16), pad)
    kr = jnp.pad(kr.astype(BF16), pad)
    keys = jnp.concatenate([ckv] + [kr] * HEADS_PER_ROPE_BLOCK, axis=2)
    keys = keys.reshape(b, n, ATT_TK, QK_WIDTH)
    vals_t = ckv.reshape(b, n, ATT_TK, KV_LORA).transpose(0, 1, 3, 2)
    return keys, vals_t


def kernel(x_prompt, x_sample, cache_ckv, cache_krope, ffn1_norm, ffn1_w_gu, ffn1_w_down, mix_norm, ffn2_norm, ffn2_w_gu, ffn2_w_down, a_w_in, a_v_norm, a_w_s, a_b_s, a_w_out, kv_norm, w_dkv, ckv_norm, w_uk, w_uv, b_w_dq, b_q_norm, b_w_uq, b_w_o, final_norm):
    n_b = DEPTH - N_A
    row = lambda v: v.reshape(1, -1)

    def ffn_weights(w_gu, w_down):
        g = w_gu[:, :D_FF].reshape(D_MODEL, N_FF_CHUNKS, FF_CHUNK)
        u = w_gu[:, D_FF:].reshape(D_MODEL, N_FF_CHUNKS, FF_CHUNK)
        gu = jnp.concatenate([g, u], axis=2).reshape(D_MODEL, 2 * D_FF)
        return gu.astype(BF16), w_down.astype(BF16)

    ffn1 = [ffn_weights(ffn1_w_gu[l], ffn1_w_down[l]) for l in range(DEPTH)]
    ffn2 = [ffn_weights(ffn2_w_gu[l], ffn2_w_down[l]) for l in range(DEPTH)]
    a_in = a_w_in.astype(BF16)
    a_out = a_w_out.astype(BF16)

    w_c = w_dkv[:, :KV_LORA].astype(BF16)
    w_kr = w_dkv[:, KV_LORA:]
    w_kr_rot = _rot_half_cols(w_kr).astype(BF16)
    w_kr = w_kr.astype(BF16)

    w_uq = b_w_uq.reshape(n_b, Q_LORA, N_HEADS, QK_NOPE + QK_ROPE)
    w_qn = w_uq[..., :QK_NOPE].reshape(n_b, Q_LORA, N_HEADS * QK_NOPE)
    w_qn_t = w_qn.transpose(0, 2, 1).astype(BF16)
    w_qr = w_uq[..., QK_NOPE:].reshape(n_b, Q_LORA, N_HEADS * QK_ROPE)
    w_qr_rot_t = jnp.stack([_rot_half_cols(w_qr[j]).T for j in range(n_b)]).astype(BF16)
    w_qr_t = w_qr.transpose(0, 2, 1).astype(BF16)
    w_dq = b_w_dq.astype(BF16)
    uk_heads = w_uk.reshape(KV_LORA, N_HEADS, QK_NOPE).transpose(1, 2, 0)
    w_absorb_t = _block_diag_pairs(uk_heads).transpose(0, 2, 1).astype(BF16)
    uv_heads = w_uv.reshape(KV_LORA, N_HEADS, V_DIM).transpose(1, 0, 2)
    w_expand = _block_diag_pairs(uv_heads).astype(BF16)
    w_o = b_w_o.astype(BF16)

    def run(x, pos, chunk, past_ckv, past_krope):
        batch, s, _ = x.shape
        causal = past_ckv is None
        h = x.reshape(batch * s, D_MODEL)
        cos, sin = _rope_tables(pos)
        lane_pad = ((0, 0), (0, max(ATT_TQ - s, 0)))
        cos_t = jnp.pad(jnp.concatenate([cos] * HEADS_PER_ROPE_BLOCK, axis=1).T, lane_pad)
        sin_t = jnp.pad(jnp.concatenate([sin] * HEADS_PER_ROPE_BLOCK, axis=1).T, lane_pad)
        v_rows = []
        ckv = kr = keys = vals_t = None
        n_keys = s
        for l in range(DEPTH):
            h = _ffn(h, row(ffn1_norm[l]), *ffn1[l])
            if l < N_A:
                bias = jnp.repeat(a_b_s[l][:, :chunk].T, A_GROUP_DIM, axis=1)
                h, *v = _gmlp(h, row(mix_norm[l]), a_in[l], row(a_v_norm[l]),
                              a_w_s[l][:, :chunk, :chunk], bias, a_out[l], chunk,
                              emit_v=not causal)
                v_rows.extend(vi.reshape(batch, s, D_MODEL) for vi in v)
            else:
                j = l - N_A
                qt = _qproj(h, row(mix_norm[l]), w_dq[j], row(b_q_norm[j]), w_qn_t[j],
                            w_absorb_t, w_qr_t[j], w_qr_rot_t[j], cos_t, sin_t, batch)
                a = _attention(qt, keys, vals_t, s, causal, n_keys)
                h = _oproj(h, a.reshape(batch * s, N_HEADS * KV_LORA), w_expand, w_o[j])
            h = _ffn(h, row(ffn2_norm[l]), *ffn2[l],
                     final_g=row(final_norm) if l == DEPTH - 1 else None)
            if l == N_A - 1:
                ckv, kr = _latent(h, row(kv_norm), w_c, w_kr, w_kr_rot, row(ckv_norm), cos, sin)
                ckv = ckv.reshape(batch, s, KV_LORA)
                kr = kr.reshape(batch, s, QK_ROPE)
                if causal:
                    ckv_all, kr_all = ckv, kr
                else:
                    ckv_all = jnp.concatenate([past_ckv, ckv], axis=1)
                    kr_all = jnp.concatenate([past_krope, kr], axis=1)
                n_keys = ckv_all.shape[1]
                keys, vals_t = _key_value_tiles(ckv_all, kr_all)
        return h.reshape(batch, s, D_MODEL), ckv, kr, v_rows

    s_p = x_prompt.shape[1]
    y_p, ckv_p, kr_p, _ = run(x_prompt, jnp.arange(s_p), A_CHUNK, None, None)
    past = cache_ckv.shape[1]
    s_s = x_sample.shape[1]
    y_s, ckv_s, kr_s, a_rows = run(x_sample, past + jnp.arange(s_s), s_s, cache_ckv, cache_krope)
    return (y_p, y_s, ckv_p, kr_p, ckv_s, kr_s, jnp.stack(a_rows, axis=0))
```

```python
import functools
import math

import jax
import jax.numpy as jnp
from jax import lax
from jax.experimental import pallas as pl
from jax.experimental.pallas import tpu as pltpu

D_MODEL = 1024
DEPTH = 4
N_A = DEPTH // 2
CHUNK = 64
A_CHUNK = 128
A_GROUPS = 8
A_GROUP_DIM = D_MODEL // A_GROUPS
N_HEADS = 16
QK_NOPE = 64
QK_ROPE = 32
V_DIM = 64
KV_LORA = 128
Q_LORA = 256
D_FF = 2816
ROPE_BASE = 10000.0
EPS = 1e-6
ATT_SCALE = (QK_NOPE + QK_ROPE) ** -0.5
Q_SCALE = ATT_SCALE * math.log2(math.e)

LANES = 128
MXU_DIM = 256
VMEM_LIMIT = 56 * 1024 * 1024

FF_CHUNK = MXU_DIM
N_FF_CHUNKS = D_FF // FF_CHUNK
HEADS_PER_ROPE_BLOCK = LANES // QK_ROPE
QK_WIDTH = KV_LORA + LANES
ATT_TQ = LANES
ATT_TK = 2 * MXU_DIM
BF16_SUBLANES = 16
V_ROWS = KV_LORA + BF16_SUBLANES

BF16 = jnp.bfloat16
F32 = jnp.float32


def _dot(a, b):
    return jnp.dot(a, b, preferred_element_type=F32)


def _rms(x, g):
    return x * lax.rsqrt(jnp.mean(x * x, axis=-1, keepdims=True) + EPS) * g


def _const_spec(shape):
    nd = len(shape)
    return pl.BlockSpec(shape, lambda *_: (0,) * nd, pipeline_mode=pl.Buffered(1))


def _params(n_axes=1, flags=None):
    return pltpu.CompilerParams(
        dimension_semantics=("parallel",) * n_axes, vmem_limit_bytes=VMEM_LIMIT, flags=flags)


def _ffn_kernel(h_ref, g_ref, wgu_ref, wd_ref, *rest, final_norm):
    if final_norm:
        fg_ref, o_ref = rest
    else:
        (o_ref,) = rest
    x = h_ref[...]
    xn = _rms(x, g_ref[...]).astype(BF16)
    acc = jnp.zeros(x.shape, F32)
    for c in range(N_FF_CHUNKS):
        g = _dot(xn, wgu_ref[:, c * FF_CHUNK:(c + 1) * FF_CHUNK])
        u = _dot(xn, wgu_ref[:, D_FF + c * FF_CHUNK:D_FF + (c + 1) * FF_CHUNK])
        a = (g * jax.nn.sigmoid(g) * u).astype(BF16)
        acc = acc + _dot(a, wd_ref[c * FF_CHUNK:(c + 1) * FF_CHUNK, :])
    y = x + 0.5 * acc
    if final_norm:
        y = _rms(y, fg_ref[...])
    o_ref[...] = y


def _layer_spec(shape, layer):
    nd = len(shape) - 1
    return pl.BlockSpec((None,) + tuple(shape[1:]), lambda *_: (layer,) + (0,) * nd,
                        pipeline_mode=pl.Buffered(1))


def _ffn(h, g, wgu, wd, layer, final_g=None, tm=512):
    t = h.shape[0]
    tm = min(tm, t)
    tok = pl.BlockSpec((tm, D_MODEL), lambda i: (i, 0))
    in_specs = [tok, _const_spec((1, D_MODEL)), _layer_spec(wgu.shape, layer),
                _layer_spec(wd.shape, layer)]
    args = [h, g, wgu, wd]
    if final_g is not None:
        in_specs.append(_const_spec((1, D_MODEL)))
        args.append(final_g)
    return pl.pallas_call(
        functools.partial(_ffn_kernel, final_norm=final_g is not None),
        grid=(t // tm,),
        in_specs=in_specs,
        out_specs=tok,
        out_shape=jax.ShapeDtypeStruct((t, D_MODEL), F32),
        compiler_params=_params(),
        name="ffn",
    )(*args)


def _gmlp_kernel(h_ref, g_ref, win_ref, vg_ref, ws_ref, bias_ref, wout_ref,
                 o_ref, *rest, chunk, emit_v):
    if emit_v:
        v_ref, gate_ref = rest
    else:
        (gate_ref,) = rest
    x = h_ref[...]
    hn = _rms(x, g_ref[...]).astype(BF16)
    z = _dot(hn, win_ref[...])
    z = 0.5 * z * (1.0 + lax.erf(z * math.sqrt(0.5)))
    u = z[:, :D_MODEL]
    v = _rms(z[:, D_MODEL:], vg_ref[...])
    if emit_v:
        v_ref[...] = v
    vb = v.astype(BF16)
    row = lax.broadcasted_iota(jnp.int32, (chunk, chunk), 0)
    col = lax.broadcasted_iota(jnp.int32, (chunk, chunk), 1)
    lower = row >= col
    w = [jnp.where(lower, ws_ref[gi], 0.0).astype(BF16) for gi in range(A_GROUPS)]
    bias = bias_ref[...]
    for c in range(x.shape[0] // chunk):
        rows = slice(c * chunk, (c + 1) * chunk)
        for gi in range(A_GROUPS):
            cols = slice(gi * A_GROUP_DIM, (gi + 1) * A_GROUP_DIM)
            sv = _dot(w[gi], vb[rows, cols]) + bias[:, cols]
            gate_ref[rows, cols] = (u[rows, cols] * sv).astype(BF16)
    o_ref[...] = x + _dot(gate_ref[...], wout_ref[...])


def _gmlp(h, g, w_in, v_g, w_s, bias, w_out, chunk, emit_v, tm=256):
    t = h.shape[0]
    tm = min(tm, t)
    tok = pl.BlockSpec((tm, D_MODEL), lambda i: (i, 0))
    n_out = 2 if emit_v else 1
    return pl.pallas_call(
        functools.partial(_gmlp_kernel, chunk=chunk, emit_v=emit_v),
        grid=(t // tm,),
        in_specs=[tok, _const_spec((1, D_MODEL)), _const_spec(w_in.shape),
                  _const_spec((1, D_MODEL)), _const_spec(w_s.shape),
                  _const_spec(bias.shape), _const_spec(w_out.shape)],
        out_specs=[tok] * n_out,
        out_shape=[jax.ShapeDtypeStruct((t, D_MODEL), F32)] * n_out,
        scratch_shapes=[pltpu.VMEM((tm, D_MODEL), BF16)],
        compiler_params=_params(),
        name="gmlp",
    )(h, g, w_in, v_g, w_s, bias, w_out)


def _latent_kernel(h_ref, g_ref, wc_ref, wr_ref, wrr_ref, cg_ref, cos_ref, sin_ref,
                   ckv_ref, kr_ref, *kv_refs):
    hn = _rms(h_ref[...], g_ref[...]).astype(BF16)
    ckv = _rms(_dot(hn, wc_ref[...]), cg_ref[...])
    kr4 = _dot(hn, wr_ref[...]) * cos_ref[...] + _dot(hn, wrr_ref[...]) * sin_ref[...]
    ckv_ref[...] = ckv
    kr_ref[...] = kr4[:, :QK_ROPE]
    if kv_refs:
        keys_ref, vals_ref = kv_refs
        keys_ref[0, 0, :, :KV_LORA] = ckv.astype(BF16)
        keys_ref[0, 0, :, KV_LORA:] = kr4.astype(BF16)
        vals_ref[0, 0, :KV_LORA, :] = ckv.T.astype(BF16)
        vals_ref[0, 0, KV_LORA:, :] = jnp.ones((V_ROWS - KV_LORA, ATT_TK), BF16)


def _latent(h, g, w_c, w_r4, w_rr4, c_g, cos4, sin4, batch, emit_kv):
    t = h.shape[0]
    s = t // batch
    tm = min(ATT_TK, s)
    assert s % tm == 0 and (tm == ATT_TK or not emit_kv)
    n_pos = s // tm
    tok = lambda w: pl.BlockSpec((tm, w), lambda i: (i, 0))
    pos = pl.BlockSpec((tm, LANES), lambda i: (i % n_pos, 0))
    out_specs = [tok(KV_LORA), tok(QK_ROPE)]
    out_shape = [jax.ShapeDtypeStruct((t, KV_LORA), F32),
                 jax.ShapeDtypeStruct((t, QK_ROPE), F32)]
    if emit_kv:
        tile = lambda r, c: pl.BlockSpec((1, 1, r, c), lambda i: (i // n_pos, i % n_pos, 0, 0))
        out_specs += [tile(ATT_TK, QK_WIDTH), tile(V_ROWS, ATT_TK)]
        out_shape += [jax.ShapeDtypeStruct((batch, n_pos, ATT_TK, QK_WIDTH), BF16),
                      jax.ShapeDtypeStruct((batch, n_pos, V_ROWS, ATT_TK), BF16)]
    return pl.pallas_call(
        _latent_kernel,
        grid=(t // tm,),
        in_specs=[tok(D_MODEL), _const_spec((1, D_MODEL)), _const_spec(w_c.shape),
                  _const_spec(w_r4.shape), _const_spec(w_rr4.shape),
                  _const_spec((1, KV_LORA)), pos, pos],
        out_specs=out_specs,
        out_shape=out_shape,
        compiler_params=_params(),
        name="latent",
    )(h, g, w_c, w_r4, w_rr4, c_g, cos4, sin4)


def _qproj_kernel(h_ref, g_ref, wdq_ref, qg_ref, wn_ref, wk_ref, wr_ref, wrr_ref,
                  cos_ref, sin_ref, qt_ref):
    tm = h_ref.shape[0]
    cols = cos_ref.shape[1]
    hn = _rms(h_ref[...], g_ref[...]).astype(BF16)
    qc = _rms(_dot(hn, wdq_ref[...]), qg_ref[...])
    if tm < cols:
        qc = jnp.concatenate([qc, jnp.zeros((cols - tm, Q_LORA), F32)], axis=0)
    qc_t = qc.T.astype(BF16)
    qn_t = _dot(wn_ref[...], qc_t).astype(BF16)
    n_blocks = N_HEADS // HEADS_PER_ROPE_BLOCK
    cos = jnp.concatenate([cos_ref[...]] * n_blocks, axis=0)
    sin = jnp.concatenate([sin_ref[...]] * n_blocks, axis=0)
    qr_t = (_dot(wr_ref[...], qc_t) * cos + _dot(wrr_ref[...], qc_t) * sin) * Q_SCALE
    n_tiles = cols // ATT_TQ

    def put(hd, d0, x):
        for i in range(n_tiles):
            qt_ref[0, i, d0:d0 + LANES, hd * ATT_TQ:(hd + 1) * ATT_TQ] = (
                x[:, i * ATT_TQ:(i + 1) * ATT_TQ].astype(BF16))

    for p in range(N_HEADS // 2):
        qa_t = _dot(wk_ref[p], qn_t[p * LANES:(p + 1) * LANES]) * Q_SCALE
        put(2 * p, 0, qa_t[:KV_LORA])
        put(2 * p + 1, 0, qa_t[KV_LORA:])
    row_head = lax.broadcasted_iota(jnp.int32, (LANES, cols), 0) // QK_ROPE
    for hd in range(N_HEADS):
        blk = hd // HEADS_PER_ROPE_BLOCK
        own = row_head == hd % HEADS_PER_ROPE_BLOCK
        put(hd, KV_LORA, jnp.where(own, qr_t[blk * LANES:(blk + 1) * LANES], 0.0))


def _qproj(h, g, w_dq, q_g, w_n, w_k, w_r, w_rr, cos_t, sin_t, batch, tm=256):
    t = h.shape[0]
    s = t // batch
    tm = min(tm, s)
    n_s = s // tm
    cols = max(tm, ATT_TQ)
    n_tiles = cols // ATT_TQ
    tok = pl.BlockSpec((tm, D_MODEL), lambda b, i: (b * n_s + i, 0))
    pos = pl.BlockSpec((LANES, cols), lambda b, i: (0, i))
    return pl.pallas_call(
        _qproj_kernel,
        grid=(batch, n_s),
        in_specs=[tok, _const_spec((1, D_MODEL)), _const_spec(w_dq.shape),
                  _const_spec((1, Q_LORA)), _const_spec(w_n.shape), _const_spec(w_k.shape),
                  _const_spec(w_r.shape), _const_spec(w_rr.shape), pos, pos],
        out_specs=pl.BlockSpec((1, n_tiles, QK_WIDTH, N_HEADS * ATT_TQ),
                               lambda b, i: (b, i, 0, 0)),
        out_shape=jax.ShapeDtypeStruct(
            (batch, n_s * n_tiles, QK_WIDTH, N_HEADS * ATT_TQ), BF16),
        compiler_params=_params(2),
        name="qproj",
    )(h, g, w_dq, q_g, w_n, w_k, w_r, w_rr, cos_t, sin_t)


def _attn_kernel(qt_ref, k_ref, vt_ref, o_ref, s0_ref, s1_ref, smax_ref, m_ref, acc_ref,
                 *, causal, n_keys):
    n_cols = N_HEADS * ATT_TQ
    m_ref[...] = jnp.full(m_ref.shape, -jnp.inf, F32)
    acc_ref[...] = jnp.zeros(acc_ref.shape, F32)

    if causal:
        q_start = pl.program_id(1) * ATT_TQ
        n_full = q_start // ATT_TK
        q_pos = q_start + lax.broadcasted_iota(jnp.int32, (1, n_cols), 1) % ATT_TQ
        limit = (q_pos // CHUNK + 1) * CHUNK
    else:
        n_full = (n_keys - 1) // ATT_TK
        limit = n_keys

    s_refs = (s0_ref, s1_ref)

    def scores(kt, slot):
        s = _dot(k_ref[0, kt], qt_ref[0, 0])
        s_refs[slot][:, :n_cols] = s
        smax_ref[slot:slot + 1, :] = jnp.max(s, axis=0, keepdims=True)

    def softmax_pv(kt, slot, n_rows=None):
        s_ref = s_refs[slot]
        if n_rows is None:
            n_rows = ATT_TK
            smax = smax_ref[slot:slot + 1, :]
        else:
            k_pos = kt * ATT_TK + lax.broadcasted_iota(jnp.int32, (n_rows, 1), 0)
            s = jnp.where(k_pos < limit, s_ref[:n_rows, :n_cols], -jnp.inf)
            s_ref[:n_rows, :n_cols] = s
            smax = jnp.max(s, axis=0, keepdims=True)
        m_prev = m_ref[...]
        m_new = jnp.maximum(m_prev, smax)
        m_ref[...] = m_new
        p = jnp.exp2(s_ref[:n_rows, :n_cols] - m_new)
        alpha = jnp.exp2(m_prev - m_new)
        acc_ref[...] = acc_ref[...] * alpha + _dot(vt_ref[0, kt, :, :n_rows], p.astype(BF16))

    def last_tile(slot):
        if causal:
            blocks = ATT_TK // ATT_TQ
            r = (q_start // ATT_TQ) % blocks
            for k in range(blocks):
                pl.when(r == k)(functools.partial(softmax_pv, n_full, slot, (k + 1) * ATT_TQ))
        else:
            tail = n_keys - n_full * ATT_TK
            softmax_pv(n_full, slot, -(-tail // LANES) * LANES)

    def pair_body(i, carry):
        kt = 2 * i
        scores(kt + 1, 1)
        softmax_pv(kt, 0)
        scores(kt + 2, 0)
        softmax_pv(kt + 1, 1)
        return carry

    scores(0, 0)
    lax.fori_loop(0, n_full // 2, pair_body, 0)
    if causal:
        odd = n_full % 2 == 1

        @pl.when(odd)
        def _():
            scores(n_full, 1)
            softmax_pv(n_full - 1, 0)
            last_tile(1)

        @pl.when(jnp.logical_not(odd))
        def _():
            last_tile(0)
    elif n_full % 2 == 1:
        scores(n_full, 1)
        softmax_pv(n_full - 1, 0)
        last_tile(1)
    else:
        last_tile(0)

    o = acc_ref[:KV_LORA, :] / acc_ref[KV_LORA:KV_LORA + 1, :]
    rows_out = o_ref.shape[1]
    for hd in range(N_HEADS):
        o_hd = o[:, hd * ATT_TQ:(hd + 1) * ATT_TQ].T
        o_ref[0, :, hd * KV_LORA:(hd + 1) * KV_LORA] = o_hd[:rows_out].astype(BF16)


def _attention(qt, keys, vals_t, s, causal, n_keys, s_pad=0):
    batch, n_q = qt.shape[:2]
    n_kt = keys.shape[1]
    rows_out = min(s, ATT_TQ)
    n_cols = N_HEADS * ATT_TQ
    return pl.pallas_call(
        functools.partial(_attn_kernel, causal=causal, n_keys=n_keys),
        grid=(batch, n_q),
        in_specs=[pl.BlockSpec((1, 1, QK_WIDTH, n_cols), lambda b, i: (b, i, 0, 0)),
                  pl.BlockSpec((1, n_kt, ATT_TK, QK_WIDTH), lambda b, i: (b, 0, 0, 0)),
                  pl.BlockSpec((1, n_kt, V_ROWS, ATT_TK), lambda b, i: (b, 0, 0, 0))],
        out_specs=pl.BlockSpec((1, rows_out, N_HEADS * KV_LORA), lambda b, i: (b, i, 0)),
        out_shape=jax.ShapeDtypeStruct((batch, s, N_HEADS * KV_LORA), BF16),
        scratch_shapes=[pltpu.VMEM((ATT_TK, n_cols + s_pad), F32),
                        pltpu.VMEM((ATT_TK, n_cols + s_pad), F32),
                        pltpu.VMEM((2, n_cols), F32), pltpu.VMEM((1, n_cols), F32),
                        pltpu.VMEM((V_ROWS, n_cols), F32)],
        compiler_params=_params(2),
        name="attention",
    )(qt, keys, vals_t)


def _oproj_kernel(h_ref, a_ref, wv_ref, wo_ref, o_ref, cat_ref):
    for p in range(N_HEADS // 2):
        cat_ref[:, p * LANES:(p + 1) * LANES] = _dot(
            a_ref[:, p * 2 * KV_LORA:(p + 1) * 2 * KV_LORA], wv_ref[p]).astype(BF16)
    o_ref[...] = h_ref[...] + _dot(cat_ref[...], wo_ref[...])


def _oproj(h, a, w_v, w_o, tm=512):
    t = h.shape[0]
    tm = min(tm, t)
    tok = lambda w: pl.BlockSpec((tm, w), lambda i: (i, 0))
    return pl.pallas_call(
        _oproj_kernel,
        grid=(t // tm,),
        in_specs=[tok(D_MODEL), tok(N_HEADS * KV_LORA), _const_spec(w_v.shape),
                  _const_spec(w_o.shape)],
        out_specs=tok(D_MODEL),
        out_shape=jax.ShapeDtypeStruct((t, D_MODEL), F32),
        scratch_shapes=[pltpu.VMEM((tm, D_MODEL), BF16)],
        compiler_params=_params(),
        name="oproj",
    )(h, a, w_v, w_o)


def _block_diag_pairs(w):
    n2, r, c = w.shape
    w = w.reshape(n2 // 2, 2, r, c)
    z = jnp.zeros_like(w[:, 0])
    top = jnp.concatenate([w[:, 0], z], axis=2)
    bot = jnp.concatenate([z, w[:, 1]], axis=2)
    return jnp.concatenate([top, bot], axis=1)


def _rot_half_cols(w):
    shp = w.shape
    w = w.reshape(shp[0], -1, 2, QK_ROPE // 2)
    return jnp.stack([-w[:, :, 1], w[:, :, 0]], axis=2).reshape(shp)


def _rope_tables(pos):
    half = QK_ROPE // 2
    inv = 1.0 / (ROPE_BASE ** (jnp.arange(half, dtype=F32) * (2.0 / QK_ROPE)))
    ang = pos.astype(F32)[:, None] * inv[None, :]
    cos = jnp.cos(ang)
    sin = jnp.sin(ang)
    return jnp.concatenate([cos, cos], axis=1), jnp.concatenate([sin, sin], axis=1)


def _key_value_tiles(ckv, kr):
    b, k, _ = ckv.shape
    n = -(-k // ATT_TK)
    pad = ((0, 0), (0, n * ATT_TK - k), (0, 0))
    ckv = jnp.pad(ckv.astype(BF16), pad)
    kr = jnp.pad(kr.astype(BF16), pad)
    keys = jnp.concatenate([ckv] + [kr] * HEADS_PER_ROPE_BLOCK, axis=2)
    keys = keys.reshape(b, n, ATT_TK, QK_WIDTH)
    vals_t = ckv.reshape(b, n, ATT_TK, KV_LORA).transpose(0, 1, 3, 2)
    ones = jnp.ones((b, n, V_ROWS - KV_LORA, ATT_TK), BF16)
    return keys, jnp.concatenate([vals_t, ones], axis=2)


def kernel(x_prompt, x_sample, cache_ckv, cache_krope, ffn1_norm, ffn1_w_gu, ffn1_w_down, mix_norm, ffn2_norm, ffn2_w_gu, ffn2_w_down, a_w_in, a_v_norm, a_w_s, a_b_s, a_w_out, kv_norm, w_dkv, ckv_norm, w_uk, w_uv, b_w_dq, b_q_norm, b_w_uq, b_w_o, final_norm):
    n_b = DEPTH - N_A
    row = lambda v: v.reshape(1, -1)

    ffn1 = (ffn1_w_gu.astype(BF16), ffn1_w_down.astype(BF16))
    ffn2 = (ffn2_w_gu.astype(BF16), ffn2_w_down.astype(BF16))
    a_in = a_w_in.astype(BF16)
    a_out = a_w_out.astype(BF16)

    w_c = w_dkv[:, :KV_LORA].astype(BF16)
    w_kr = w_dkv[:, KV_LORA:]
    w_kr4 = jnp.tile(w_kr, (1, HEADS_PER_ROPE_BLOCK)).astype(BF16)
    w_kr_rot4 = jnp.tile(_rot_half_cols(w_kr), (1, HEADS_PER_ROPE_BLOCK)).astype(BF16)

    w_uq = b_w_uq.reshape(n_b, Q_LORA, N_HEADS, QK_NOPE + QK_ROPE)
    w_qn = w_uq[..., :QK_NOPE].reshape(n_b, Q_LORA, N_HEADS * QK_NOPE)
    w_qn_t = w_qn.transpose(0, 2, 1).astype(BF16)
    w_qr = w_uq[..., QK_NOPE:].reshape(n_b, Q_LORA, N_HEADS * QK_ROPE)
    w_qr_rot_t = jnp.stack([_rot_half_cols(w_qr[j]).T for j in range(n_b)]).astype(BF16)
    w_qr_t = w_qr.transpose(0, 2, 1).astype(BF16)
    w_dq = b_w_dq.astype(BF16)
    uk_heads = w_uk.reshape(KV_LORA, N_HEADS, QK_NOPE).transpose(1, 2, 0)
    w_absorb_t = _block_diag_pairs(uk_heads).transpose(0, 2, 1).astype(BF16)
    uv_heads = w_uv.reshape(KV_LORA, N_HEADS, V_DIM).transpose(1, 0, 2)
    w_expand = _block_diag_pairs(uv_heads).astype(BF16)
    w_o = b_w_o.astype(BF16)

    def run(x, pos, chunk, past_ckv, past_krope):
        batch, s, _ = x.shape
        causal = past_ckv is None
        h = x.reshape(batch * s, D_MODEL)
        cos, sin = _rope_tables(pos)
        cos4 = jnp.concatenate([cos] * HEADS_PER_ROPE_BLOCK, axis=1)
        sin4 = jnp.concatenate([sin] * HEADS_PER_ROPE_BLOCK, axis=1)
        lane_pad = ((0, 0), (0, max(ATT_TQ - s, 0)))
        cos_t = jnp.pad(cos4.T, lane_pad)
        sin_t = jnp.pad(sin4.T, lane_pad)
        v_rows = []
        ckv = kr = keys = vals_t = None
        n_keys = s
        for l in range(DEPTH):
            h = _ffn(h, row(ffn1_norm[l]), *ffn1, l, tm=1024)
            if l < N_A:
                bias = jnp.repeat(a_b_s[l][:, :chunk].T, A_GROUP_DIM, axis=1)
                h, *v = _gmlp(h, row(mix_norm[l]), a_in[l], row(a_v_norm[l]),
                              a_w_s[l][:, :chunk, :chunk], bias, a_out[l], chunk,
                              emit_v=not causal, tm=1024)
                v_rows.extend(vi.reshape(batch, s, D_MODEL) for vi in v)
            else:
                j = l - N_A
                qt = _qproj(h, row(mix_norm[l]), w_dq[j], row(b_q_norm[j]), w_qn_t[j],
                            w_absorb_t, w_qr_t[j], w_qr_rot_t[j], cos_t, sin_t, batch, tm=1024)
                a = _attention(qt, keys, vals_t, s, causal, n_keys, s_pad=LANES if j else 0)
                h = _oproj(h, a.reshape(batch * s, N_HEADS * KV_LORA), w_expand, w_o[j],
                           tm=1024 if j else 512)
            h = _ffn(h, row(ffn2_norm[l]), *ffn2, l,
                     final_g=row(final_norm) if l == DEPTH - 1 else None, tm=1024)
            if l == N_A - 1:
                ckv, kr, *kv = _latent(h, row(kv_norm), w_c, w_kr4, w_kr_rot4, row(ckv_norm),
                                       cos4, sin4, batch, emit_kv=causal)
                ckv = ckv.reshape(batch, s, KV_LORA)
                kr = kr.reshape(batch, s, QK_ROPE)
                if causal:
                    keys, vals_t = kv
                else:
                    ckv_all = jnp.concatenate([past_ckv, ckv], axis=1)
                    kr_all = jnp.concatenate([past_krope, kr], axis=1)
                    n_keys = ckv_all.shape[1]
                    keys, vals_t = _key_value_tiles(ckv_all, kr_all)
        return h.reshape(batch, s, D_MODEL), ckv, kr, v_rows

    s_p = x_prompt.shape[1]
    y_p, ckv_p, kr_p, _ = run(x_prompt, jnp.arange(s_p), A_CHUNK, None, None)
    past = cache_ckv.shape[1]
    s_s = x_sample.shape[1]
    y_s, ckv_s, kr_s, a_rows = run(x_sample, past + jnp.arange(s_s), s_s, cache_ckv, cache_krope)
    return (y_p, y_s, ckv_p, kr_p, ckv_s, kr_s, jnp.stack(a_rows, axis=0))
```

```python
import functools
import math

import jax
import jax.numpy as jnp
from jax import lax
from jax.experimental import pallas as pl
from jax.experimental.pallas import tpu as pltpu

D_MODEL = 1024
DEPTH = 4
N_A = DEPTH // 2
CHUNK = 64
A_CHUNK = 128
A_GROUPS = 8
A_GROUP_DIM = D_MODEL // A_GROUPS
N_HEADS = 16
QK_NOPE = 64
QK_ROPE = 32
V_DIM = 64
KV_LORA = 128
Q_LORA = 256
D_FF = 2816
ROPE_BASE = 10000.0
EPS = 1e-6
ATT_SCALE = (QK_NOPE + QK_ROPE) ** -0.5
Q_SCALE = ATT_SCALE * math.log2(math.e)

LANES = 128
MXU_DIM = 256
VMEM_LIMIT = 56 * 1024 * 1024

FF_CHUNK = MXU_DIM
N_FF_CHUNKS = D_FF // FF_CHUNK
HEADS_PER_ROPE_BLOCK = LANES // QK_ROPE
QK_WIDTH = KV_LORA + LANES
ATT_TQ = LANES
ATT_TK = 2 * MXU_DIM
BF16_SUBLANES = 16
V_ROWS = KV_LORA + BF16_SUBLANES

BF16 = jnp.bfloat16
F32 = jnp.float32


def _dot(a, b):
    return jnp.dot(a, b, preferred_element_type=F32)


def _rms(x, g):
    return x * lax.rsqrt(jnp.mean(x * x, axis=-1, keepdims=True) + EPS) * g


def _const_spec(shape):
    nd = len(shape)
    return pl.BlockSpec(shape, lambda *_: (0,) * nd, pipeline_mode=pl.Buffered(1))


def _params(n_axes=1, flags=None):
    return pltpu.CompilerParams(
        dimension_semantics=("parallel",) * n_axes, vmem_limit_bytes=VMEM_LIMIT, flags=flags)


def _ffn_kernel(h_ref, g_ref, wgu_ref, wd_ref, *rest, final_norm):
    if final_norm:
        fg_ref, o_ref = rest
    else:
        (o_ref,) = rest
    x = h_ref[...]
    xn = _rms(x, g_ref[...]).astype(BF16)
    acc = jnp.zeros(x.shape, F32)
    for c in range(N_FF_CHUNKS):
        g = _dot(xn, wgu_ref[:, c * FF_CHUNK:(c + 1) * FF_CHUNK])
        u = _dot(xn, wgu_ref[:, D_FF + c * FF_CHUNK:D_FF + (c + 1) * FF_CHUNK])
        a = (g * jax.nn.sigmoid(g) * u).astype(BF16)
        acc = acc + _dot(a, wd_ref[c * FF_CHUNK:(c + 1) * FF_CHUNK, :])
    y = x + 0.5 * acc
    if final_norm:
        y = _rms(y, fg_ref[...])
    o_ref[...] = y


def _layer_spec(shape, layer):
    nd = len(shape) - 1
    return pl.BlockSpec((None,) + tuple(shape[1:]), lambda *_: (layer,) + (0,) * nd,
                        pipeline_mode=pl.Buffered(1))


def _ffn(h, g, wgu, wd, layer, final_g=None, tm=512):
    t = h.shape[0]
    tm = min(tm, t)
    tok = pl.BlockSpec((tm, D_MODEL), lambda i: (i, 0))
    in_specs = [tok, _const_spec((1, D_MODEL)), _layer_spec(wgu.shape, layer),
                _layer_spec(wd.shape, layer)]
    args = [h, g, wgu, wd]
    if final_g is not None:
        in_specs.append(_const_spec((1, D_MODEL)))
        args.append(final_g)
    return pl.pallas_call(
        functools.partial(_ffn_kernel, final_norm=final_g is not None),
        grid=(t // tm,),
        in_specs=in_specs,
        out_specs=tok,
        out_shape=jax.ShapeDtypeStruct((t, D_MODEL), F32),
        compiler_params=_params(),
        name="ffn",
    )(*args)


def _gmlp_kernel(h_ref, g_ref, win_ref, vg_ref, ws_ref, bias_ref, wout_ref,
                 o_ref, *rest, chunk, emit_v):
    if emit_v:
        v_ref, gate_ref = rest
    else:
        (gate_ref,) = rest
    x = h_ref[...]
    hn = _rms(x, g_ref[...]).astype(BF16)
    z = _dot(hn, win_ref[...])
    z = 0.5 * z * (1.0 + lax.erf(z * math.sqrt(0.5)))
    u = z[:, :D_MODEL]
    v = _rms(z[:, D_MODEL:], vg_ref[...])
    if emit_v:
        v_ref[...] = v
    vb = v.astype(BF16)
    row = lax.broadcasted_iota(jnp.int32, (chunk, chunk), 0)
    col = lax.broadcasted_iota(jnp.int32, (chunk, chunk), 1)
    lower = row >= col
    w = [jnp.where(lower, ws_ref[gi], 0.0).astype(BF16) for gi in range(A_GROUPS)]
    bias = bias_ref[...]
    for c in range(x.shape[0] // chunk):
        rows = slice(c * chunk, (c + 1) * chunk)
        for gi in range(A_GROUPS):
            cols = slice(gi * A_GROUP_DIM, (gi + 1) * A_GROUP_DIM)
            sv = _dot(w[gi], vb[rows, cols]) + bias[:, cols]
            gate_ref[rows, cols] = (u[rows, cols] * sv).astype(BF16)
    o_ref[...] = x + _dot(gate_ref[...], wout_ref[...])


def _gmlp(h, g, w_in, v_g, w_s, bias, w_out, chunk, emit_v, tm=256):
    t = h.shape[0]
    tm = min(tm, t)
    tok = pl.BlockSpec((tm, D_MODEL), lambda i: (i, 0))
    n_out = 2 if emit_v else 1
    return pl.pallas_call(
        functools.partial(_gmlp_kernel, chunk=chunk, emit_v=emit_v),
        grid=(t // tm,),
        in_specs=[tok, _const_spec((1, D_MODEL)), _const_spec(w_in.shape),
                  _const_spec((1, D_MODEL)), _const_spec(w_s.shape),
                  _const_spec(bias.shape), _const_spec(w_out.shape)],
        out_specs=[tok] * n_out,
        out_shape=[jax.ShapeDtypeStruct((t, D_MODEL), F32)] * n_out,
        scratch_shapes=[pltpu.VMEM((tm, D_MODEL), BF16)],
        compiler_params=_params(),
        name="gmlp",
    )(h, g, w_in, v_g, w_s, bias, w_out)


def _latent_kernel(h_ref, g_ref, wc_ref, wr_ref, wrr_ref, cg_ref, cos_ref, sin_ref,
                   ckv_ref, kr_ref, *kv_refs):
    hn = _rms(h_ref[...], g_ref[...]).astype(BF16)
    ckv = _rms(_dot(hn, wc_ref[...]), cg_ref[...])
    kr4 = _dot(hn, wr_ref[...]) * cos_ref[...] + _dot(hn, wrr_ref[...]) * sin_ref[...]
    ckv_ref[...] = ckv
    kr_ref[...] = kr4[:, :QK_ROPE]
    if kv_refs:
        keys_ref, vals_ref = kv_refs
        keys_ref[0, 0, :, :KV_LORA] = ckv.astype(BF16)
        keys_ref[0, 0, :, KV_LORA:] = kr4.astype(BF16)
        vals_ref[0, 0, :KV_LORA, :] = ckv.T.astype(BF16)
        vals_ref[0, 0, KV_LORA:, :] = jnp.ones((V_ROWS - KV_LORA, ATT_TK), BF16)


def _latent(h, g, w_c, w_r4, w_rr4, c_g, cos4, sin4, batch, emit_kv):
    t = h.shape[0]
    s = t // batch
    tm = min(ATT_TK, s)
    assert s % tm == 0 and (tm == ATT_TK or not emit_kv)
    n_pos = s // tm
    tok = lambda w: pl.BlockSpec((tm, w), lambda i: (i, 0))
    pos = pl.BlockSpec((tm, LANES), lambda i: (i % n_pos, 0))
    out_specs = [tok(KV_LORA), tok(QK_ROPE)]
    out_shape = [jax.ShapeDtypeStruct((t, KV_LORA), F32),
                 jax.ShapeDtypeStruct((t, QK_ROPE), F32)]
    if emit_kv:
        tile = lambda r, c: pl.BlockSpec((1, 1, r, c), lambda i: (i // n_pos, i % n_pos, 0, 0))
        out_specs += [tile(ATT_TK, QK_WIDTH), tile(V_ROWS, ATT_TK)]
        out_shape += [jax.ShapeDtypeStruct((batch, n_pos, ATT_TK, QK_WIDTH), BF16),
                      jax.ShapeDtypeStruct((batch, n_pos, V_ROWS, ATT_TK), BF16)]
    return pl.pallas_call(
        _latent_kernel,
        grid=(t // tm,),
        in_specs=[tok(D_MODEL), _const_spec((1, D_MODEL)), _const_spec(w_c.shape),
                  _const_spec(w_r4.shape), _const_spec(w_rr4.shape),
                  _const_spec((1, KV_LORA)), pos, pos],
        out_specs=out_specs,
        out_shape=out_shape,
        compiler_params=_params(),
        name="latent",
    )(h, g, w_c, w_r4, w_rr4, c_g, cos4, sin4)


def _qproj_kernel(h_ref, g_ref, wdq_ref, qg_ref, wn_ref, wk_ref, wr_ref, wrr_ref,
                  cos_ref, sin_ref, qt_ref):
    tm = h_ref.shape[0]
    cols = cos_ref.shape[1]
    hn = _rms(h_ref[...], g_ref[...]).astype(BF16)
    qc = _rms(_dot(hn, wdq_ref[...]), qg_ref[...])
    if tm < cols:
        qc = jnp.concatenate([qc, jnp.zeros((cols - tm, Q_LORA), F32)], axis=0)
    qc_t = qc.T.astype(BF16)
    qn_t = _dot(wn_ref[...], qc_t).astype(BF16)
    n_blocks = N_HEADS // HEADS_PER_ROPE_BLOCK
    cos = jnp.concatenate([cos_ref[...]] * n_blocks, axis=0)
    sin = jnp.concatenate([sin_ref[...]] * n_blocks, axis=0)
    qr_t = (_dot(wr_ref[...], qc_t) * cos + _dot(wrr_ref[...], qc_t) * sin) * Q_SCALE
    n_tiles = cols // ATT_TQ

    def put(hd, d0, x):
        for i in range(n_tiles):
            qt_ref[0, i, d0:d0 + LANES, hd * ATT_TQ:(hd + 1) * ATT_TQ] = (
                x[:, i * ATT_TQ:(i + 1) * ATT_TQ].astype(BF16))

    for p in range(N_HEADS // 2):
        qa_t = _dot(wk_ref[p], qn_t[p * LANES:(p + 1) * LANES]) * Q_SCALE
        put(2 * p, 0, qa_t[:KV_LORA])
        put(2 * p + 1, 0, qa_t[KV_LORA:])
    row_head = lax.broadcasted_iota(jnp.int32, (LANES, cols), 0) // QK_ROPE
    for hd in range(N_HEADS):
        blk = hd // HEADS_PER_ROPE_BLOCK
        own = row_head == hd % HEADS_PER_ROPE_BLOCK
        put(hd, KV_LORA, jnp.where(own, qr_t[blk * LANES:(blk + 1) * LANES], 0.0))


def _qproj(h, g, w_dq, q_g, w_n, w_k, w_r, w_rr, cos_t, sin_t, batch, tm=256):
    t = h.shape[0]
    s = t // batch
    tm = min(tm, s)
    n_s = s // tm
    cols = max(tm, ATT_TQ)
    n_tiles = cols // ATT_TQ
    tok = pl.BlockSpec((tm, D_MODEL), lambda b, i: (b * n_s + i, 0))
    pos = pl.BlockSpec((LANES, cols), lambda b, i: (0, i))
    return pl.pallas_call(
        _qproj_kernel,
        grid=(batch, n_s),
        in_specs=[tok, _const_spec((1, D_MODEL)), _const_spec(w_dq.shape),
                  _const_spec((1, Q_LORA)), _const_spec(w_n.shape), _const_spec(w_k.shape),
                  _const_spec(w_r.shape), _const_spec(w_rr.shape), pos, pos],
        out_specs=pl.BlockSpec((1, n_tiles, QK_WIDTH, N_HEADS * ATT_TQ),
                               lambda b, i: (b, i, 0, 0)),
        out_shape=jax.ShapeDtypeStruct(
            (batch, n_s * n_tiles, QK_WIDTH, N_HEADS * ATT_TQ), BF16),
        compiler_params=_params(2),
        name="qproj",
    )(h, g, w_dq, q_g, w_n, w_k, w_r, w_rr, cos_t, sin_t)


def _attn_kernel(qt_ref, qn_ref, k_ref, vt_ref, o_ref, s0_ref, s1_ref, s2_ref, smax_ref, m_ref,
                 acc_ref, *, causal, n_keys, prefetch):
    n_cols = N_HEADS * ATT_TQ
    step = pl.program_id(1)
    m_ref[...] = jnp.full(m_ref.shape, -jnp.inf, F32)
    acc_ref[...] = jnp.zeros(acc_ref.shape, F32)

    if causal:
        q_start = step * ATT_TQ
        n_full = q_start // ATT_TK
        q_pos = q_start + lax.broadcasted_iota(jnp.int32, (1, n_cols), 1) % ATT_TQ
        limit = (q_pos // CHUNK + 1) * CHUNK
    else:
        n_full = (n_keys - 1) // ATT_TK
        limit = n_keys

    s_refs = (s0_ref, s1_ref, s2_ref)

    def scores(kt, slot, q_ref=qt_ref):
        s = _dot(k_ref[0, kt], q_ref[0, 0])
        s_refs[slot][...] = s
        if slot < 2:
            smax_ref[slot:slot + 1, :] = jnp.max(s, axis=0, keepdims=True)

    def softmax_pv(kt, slot, n_rows=None):
        s_ref = s_refs[slot]
        if n_rows is None:
            n_rows = ATT_TK
            smax = smax_ref[slot:slot + 1, :]
        else:
            k_pos = kt * ATT_TK + lax.broadcasted_iota(jnp.int32, (n_rows, 1), 0)
            s = jnp.where(k_pos < limit, s_ref[:n_rows, :], -jnp.inf)
            s_ref[:n_rows, :] = s
            smax = jnp.max(s, axis=0, keepdims=True)
        m_prev = m_ref[...]
        m_new = jnp.maximum(m_prev, smax)
        m_ref[...] = m_new
        p = jnp.exp2(s_ref[:n_rows, :] - m_new)
        alpha = jnp.exp2(m_prev - m_new)
        acc_ref[...] = acc_ref[...] * alpha + _dot(vt_ref[0, kt, :, :n_rows], p.astype(BF16))

    def finish(slot, n_rows):
        if prefetch and slot != 0:
            scores(0, 0, qn_ref)
        softmax_pv(n_full, slot, n_rows)
        if prefetch and slot == 0:
            scores(0, 0, qn_ref)

    def last_tile(slot):
        if causal:
            blocks = ATT_TK // ATT_TQ
            r = (q_start // ATT_TQ) % blocks
            for k in range(blocks):
                pl.when(r == k)(functools.partial(finish, slot, (k + 1) * ATT_TQ))
        else:
            tail = n_keys - n_full * ATT_TK
            finish(slot, -(-tail // LANES) * LANES)

    def pair_body(i, carry):
        kt = 2 * i
        scores(kt + 1, 1)
        softmax_pv(kt, 0)
        scores(kt + 2, 0)
        softmax_pv(kt + 1, 1)
        return carry

    def tail_one():
        scores(n_full, 2)
        softmax_pv(n_full - 1, 0)

    def tail_two():
        scores(n_full - 1, 1)
        softmax_pv(n_full - 2, 0)
        scores(n_full, 2)
        softmax_pv(n_full - 1, 1)

    if prefetch:
        pl.when(step == 0)(functools.partial(scores, 0, 0))
    else:
        scores(0, 0)
    if causal:
        pl.when(n_full == 0)(functools.partial(finish, 0, ATT_TK))

        @pl.when(n_full > 0)
        def _():
            lax.fori_loop(0, (n_full - 1) // 2, pair_body, 0)
            pl.when(n_full % 2 == 1)(tail_one)
            pl.when(n_full % 2 == 0)(tail_two)
            last_tile(2)
    else:
        assert n_full > 0
        lax.fori_loop(0, (n_full - 1) // 2, pair_body, 0)
        (tail_one if n_full % 2 else tail_two)()
        last_tile(2)

    o = acc_ref[:KV_LORA, :] / acc_ref[KV_LORA:KV_LORA + 1, :]
    rows_out = o_ref.shape[1]
    for hd in range(N_HEADS):
        o_hd = o[:, hd * ATT_TQ:(hd + 1) * ATT_TQ].T
        o_ref[0, :, hd * KV_LORA:(hd + 1) * KV_LORA] = o_hd[:rows_out].astype(BF16)


def _attention(qt, keys, vals_t, s, causal, n_keys):
    batch, n_q = qt.shape[:2]
    n_kt = keys.shape[1]
    rows_out = min(s, ATT_TQ)
    n_cols = N_HEADS * ATT_TQ
    q_spec = lambda nxt: pl.BlockSpec(
        (1, 1, QK_WIDTH, n_cols), lambda b, i: (b, jnp.minimum(i + nxt, n_q - 1), 0, 0))
    score_buf = pltpu.VMEM((ATT_TK, n_cols), F32)
    return pl.pallas_call(
        functools.partial(_attn_kernel, causal=causal, n_keys=n_keys, prefetch=n_q > 1),
        grid=(batch, n_q),
        in_specs=[q_spec(0), q_spec(1),
                  pl.BlockSpec((1, n_kt, ATT_TK, QK_WIDTH), lambda b, i: (b, 0, 0, 0)),
                  pl.BlockSpec((1, n_kt, V_ROWS, ATT_TK), lambda b, i: (b, 0, 0, 0))],
        out_specs=pl.BlockSpec((1, rows_out, N_HEADS * KV_LORA), lambda b, i: (b, i, 0)),
        out_shape=jax.ShapeDtypeStruct((batch, s, N_HEADS * KV_LORA), BF16),
        scratch_shapes=[score_buf, score_buf, score_buf,
                        pltpu.VMEM((2, n_cols), F32), pltpu.VMEM((1, n_cols), F32),
                        pltpu.VMEM((V_ROWS, n_cols), F32)],
        compiler_params=pltpu.CompilerParams(
            dimension_semantics=("arbitrary", "arbitrary"), vmem_limit_bytes=VMEM_LIMIT),
        name="attention",
    )(qt, qt, keys, vals_t)


def _oproj_kernel(h_ref, a_ref, wv_ref, wo_ref, o_ref, cat_ref):
    for p in range(N_HEADS // 2):
        cat_ref[:, p * LANES:(p + 1) * LANES] = _dot(
            a_ref[:, p * 2 * KV_LORA:(p + 1) * 2 * KV_LORA], wv_ref[p]).astype(BF16)
    o_ref[...] = h_ref[...] + _dot(cat_ref[...], wo_ref[...])


def _oproj(h, a, w_v, w_o, tm=512):
    t = h.shape[0]
    tm = min(tm, t)
    tok = lambda w: pl.BlockSpec((tm, w), lambda i: (i, 0))
    return pl.pallas_call(
        _oproj_kernel,
        grid=(t // tm,),
        in_specs=[tok(D_MODEL), tok(N_HEADS * KV_LORA), _const_spec(w_v.shape),
                  _const_spec(w_o.shape)],
        out_specs=tok(D_MODEL),
        out_shape=jax.ShapeDtypeStruct((t, D_MODEL), F32),
        scratch_shapes=[pltpu.VMEM((tm, D_MODEL), BF16)],
        compiler_params=_params(),
        name="oproj",
    )(h, a, w_v, w_o)


def _block_diag_pairs(w):
    n2, r, c = w.shape
    w = w.reshape(n2 // 2, 2, r, c)
    z = jnp.zeros_like(w[:, 0])
    top = jnp.concatenate([w[:, 0], z], axis=2)
    bot = jnp.concatenate([z, w[:, 1]], axis=2)
    return jnp.concatenate([top, bot], axis=1)


def _rot_half_cols(w):
    shp = w.shape
    w = w.reshape(shp[0], -1, 2, QK_ROPE // 2)
    return jnp.stack([-w[:, :, 1], w[:, :, 0]], axis=2).reshape(shp)


def _rope_tables(pos):
    half = QK_ROPE // 2
    inv = 1.0 / (ROPE_BASE ** (jnp.arange(half, dtype=F32) * (2.0 / QK_ROPE)))
    ang = pos.astype(F32)[:, None] * inv[None, :]
    cos = jnp.cos(ang)
    sin = jnp.sin(ang)
    return jnp.concatenate([cos, cos], axis=1), jnp.concatenate([sin, sin], axis=1)


def _key_value_tiles(ckv, kr):
    b, k, _ = ckv.shape
    n = -(-k // ATT_TK)
    pad = ((0, 0), (0, n * ATT_TK - k), (0, 0))
    ckv = jnp.pad(ckv.astype(BF16), pad)
    kr = jnp.pad(kr.astype(BF16), pad)
    keys = jnp.concatenate([ckv] + [kr] * HEADS_PER_ROPE_BLOCK, axis=2)
    keys = keys.reshape(b, n, ATT_TK, QK_WIDTH)
    vals_t = ckv.reshape(b, n, ATT_TK, KV_LORA).transpose(0, 1, 3, 2)
    ones = jnp.ones((b, n, V_ROWS - KV_LORA, ATT_TK), BF16)
    return keys, jnp.concatenate([vals_t, ones], axis=2)


def kernel(x_prompt, x_sample, cache_ckv, cache_krope, ffn1_norm, ffn1_w_gu, ffn1_w_down, mix_norm, ffn2_norm, ffn2_w_gu, ffn2_w_down, a_w_in, a_v_norm, a_w_s, a_b_s, a_w_out, kv_norm, w_dkv, ckv_norm, w_uk, w_uv, b_w_dq, b_q_norm, b_w_uq, b_w_o, final_norm):
    n_b = DEPTH - N_A
    row = lambda v: v.reshape(1, -1)

    ffn1 = (ffn1_w_gu.astype(BF16), ffn1_w_down.astype(BF16))
    ffn2 = (ffn2_w_gu.astype(BF16), ffn2_w_down.astype(BF16))
    a_in = a_w_in.astype(BF16)
    a_out = a_w_out.astype(BF16)

    w_c = w_dkv[:, :KV_LORA].astype(BF16)
    w_kr = w_dkv[:, KV_LORA:]
    w_kr4 = jnp.tile(w_kr, (1, HEADS_PER_ROPE_BLOCK)).astype(BF16)
    w_kr_rot4 = jnp.tile(_rot_half_cols(w_kr), (1, HEADS_PER_ROPE_BLOCK)).astype(BF16)

    w_uq = b_w_uq.reshape(n_b, Q_LORA, N_HEADS, QK_NOPE + QK_ROPE)
    w_qn = w_uq[..., :QK_NOPE].reshape(n_b, Q_LORA, N_HEADS * QK_NOPE)
    w_qn_t = w_qn.transpose(0, 2, 1).astype(BF16)
    w_qr = w_uq[..., QK_NOPE:].reshape(n_b, Q_LORA, N_HEADS * QK_ROPE)
    w_qr_rot_t = jnp.stack([_rot_half_cols(w_qr[j]).T for j in range(n_b)]).astype(BF16)
    w_qr_t = w_qr.transpose(0, 2, 1).astype(BF16)
    w_dq = b_w_dq.astype(BF16)
    uk_heads = w_uk.reshape(KV_LORA, N_HEADS, QK_NOPE).transpose(1, 2, 0)
    w_absorb_t = _block_diag_pairs(uk_heads).transpose(0, 2, 1).astype(BF16)
    uv_heads = w_uv.reshape(KV_LORA, N_HEADS, V_DIM).transpose(1, 0, 2)
    w_expand = _block_diag_pairs(uv_heads).astype(BF16)
    w_o = b_w_o.astype(BF16)

    def run(x, pos, chunk, past_ckv, past_krope):
        batch, s, _ = x.shape
        causal = past_ckv is None
        h = x.reshape(batch * s, D_MODEL)
        cos, sin = _rope_tables(pos)
        cos4 = jnp.concatenate([cos] * HEADS_PER_ROPE_BLOCK, axis=1)
        sin4 = jnp.concatenate([sin] * HEADS_PER_ROPE_BLOCK, axis=1)
        lane_pad = ((0, 0), (0, max(ATT_TQ - s, 0)))
        cos_t = jnp.pad(cos4.T, lane_pad)
        sin_t = jnp.pad(sin4.T, lane_pad)
        v_rows = []
        ckv = kr = keys = vals_t = None
        n_keys = s
        for l in range(DEPTH):
            h = _ffn(h, row(ffn1_norm[l]), *ffn1, l, tm=1024)
            if l < N_A:
                bias = jnp.repeat(a_b_s[l][:, :chunk].T, A_GROUP_DIM, axis=1)
                h, *v = _gmlp(h, row(mix_norm[l]), a_in[l], row(a_v_norm[l]),
                              a_w_s[l][:, :chunk, :chunk], bias, a_out[l], chunk,
                              emit_v=not causal, tm=1024)
                v_rows.extend(vi.reshape(batch, s, D_MODEL) for vi in v)
            else:
                j = l - N_A
                qt = _qproj(h, row(mix_norm[l]), w_dq[j], row(b_q_norm[j]), w_qn_t[j],
                            w_absorb_t, w_qr_t[j], w_qr_rot_t[j], cos_t, sin_t, batch, tm=1024)
                a = _attention(qt, keys, vals_t, s, causal, n_keys)
                h = _oproj(h, a.reshape(batch * s, N_HEADS * KV_LORA), w_expand, w_o[j],
                           tm=1024)
            h = _ffn(h, row(ffn2_norm[l]), *ffn2, l,
                     final_g=row(final_norm) if l == DEPTH - 1 else None, tm=1024)
            if l == N_A - 1:
                ckv, kr, *kv = _latent(h, row(kv_norm), w_c, w_kr4, w_kr_rot4, row(ckv_norm),
                                       cos4, sin4, batch, emit_kv=causal)
                ckv = ckv.reshape(batch, s, KV_LORA)
                kr = kr.reshape(batch, s, QK_ROPE)
                if causal:
                    keys, vals_t = kv
                else:
                    ckv_all = jnp.concatenate([past_ckv, ckv], axis=1)
                    kr_all = jnp.concatenate([past_krope, kr], axis=1)
                    n_keys = ckv_all.shape[1]
                    keys, vals_t = _key_value_tiles(ckv_all, kr_all)
        return h.reshape(batch, s, D_MODEL), ckv, kr, v_rows

    s_p = x_prompt.shape[1]
    y_p, ckv_p, kr_p, _ = run(x_prompt, jnp.arange(s_p), A_CHUNK, None, None)
    past = cache_ckv.shape[1]
    s_s = x_sample.shape[1]
    y_s, ckv_s, kr_s, a_rows = run(x_sample, past + jnp.arange(s_s), s_s, cache_ckv, cache_krope)
    return (y_p, y_s, ckv_p, kr_p, ckv_s, kr_s, jnp.stack(a_rows, axis=0))
```

```python
import functools
import math

import jax
import jax.numpy as jnp
from jax import lax
from jax.experimental import pallas as pl
from jax.experimental.pallas import tpu as pltpu

D_MODEL = 1024
DEPTH = 4
N_A = DEPTH // 2
CHUNK = 64
A_CHUNK = 128
A_GROUPS = 8
A_GROUP_DIM = D_MODEL // A_GROUPS
N_HEADS = 16
QK_NOPE = 64
QK_ROPE = 32
V_DIM = 64
KV_LORA = 128
Q_LORA = 256
D_FF = 2816
ROPE_BASE = 10000.0
EPS = 1e-6
ATT_SCALE = (QK_NOPE + QK_ROPE) ** -0.5
Q_SCALE = ATT_SCALE * math.log2(math.e)

LANES = 128
MXU_DIM = 256
VMEM_LIMIT = 56 * 1024 * 1024

TOKEN_TILE = 1024
FF_CHUNK = MXU_DIM
HEADS_PER_ROPE_BLOCK = LANES // QK_ROPE
QK_WIDTH = KV_LORA + LANES
ATT_TQ = LANES
ATT_TK = 2 * MXU_DIM
ATT_LAST_STEP = LANES
BF16_SUBLANES = 16
V_ROWS = KV_LORA + BF16_SUBLANES

BF16 = jnp.bfloat16
F32 = jnp.float32


def _dot(a, b):
    return jnp.dot(a, b, preferred_element_type=F32)


def _rms(x, g):
    return x * lax.rsqrt(jnp.mean(x * x, axis=-1, keepdims=True) + EPS) * g


def _const_spec(shape):
    nd = len(shape)
    return pl.BlockSpec(shape, lambda *_: (0,) * nd, pipeline_mode=pl.Buffered(1))


def _params(n_axes=1):
    return pltpu.CompilerParams(
        dimension_semantics=("parallel",) * n_axes, vmem_limit_bytes=VMEM_LIMIT)


def _ffn_kernel(h_ref, g_ref, wgu_ref, wd_ref, *rest, final_norm):
    if final_norm:
        fg_ref, o_ref = rest
    else:
        (o_ref,) = rest
    x = h_ref[...]
    xn = _rms(x, g_ref[...]).astype(BF16)
    acc = jnp.zeros(x.shape, F32)
    for lo in range(0, D_FF, FF_CHUNK):
        hi = min(lo + FF_CHUNK, D_FF)
        g = _dot(xn, wgu_ref[:, lo:hi])
        u = _dot(xn, wgu_ref[:, D_FF + lo:D_FF + hi])
        a = (g * jax.nn.sigmoid(g) * u).astype(BF16)
        acc = acc + _dot(a, wd_ref[lo:hi, :])
    y = x + 0.5 * acc
    if final_norm:
        y = _rms(y, fg_ref[...])
    o_ref[...] = y


def _layer_spec(shape, layer):
    nd = len(shape) - 1
    return pl.BlockSpec((None,) + tuple(shape[1:]), lambda *_: (layer,) + (0,) * nd,
                        pipeline_mode=pl.Buffered(1))


def _ffn(h, g, wgu, wd, layer, final_g=None):
    t = h.shape[0]
    tm = min(TOKEN_TILE, t)
    tok = pl.BlockSpec((tm, D_MODEL), lambda i: (i, 0))
    in_specs = [tok, _const_spec((1, D_MODEL)), _layer_spec(wgu.shape, layer),
                _layer_spec(wd.shape, layer)]
    args = [h, g, wgu, wd]
    if final_g is not None:
        in_specs.append(_const_spec((1, D_MODEL)))
        args.append(final_g)
    return pl.pallas_call(
        functools.partial(_ffn_kernel, final_norm=final_g is not None),
        grid=(t // tm,),
        in_specs=in_specs,
        out_specs=tok,
        out_shape=jax.ShapeDtypeStruct((t, D_MODEL), F32),
        compiler_params=_params(),
        name="ffn",
    )(*args)


def _gmlp_kernel(h_ref, g_ref, win_ref, vg_ref, ws_ref, bias_ref, wout_ref,
                 o_ref, *rest, chunk, emit_v):
    if emit_v:
        v_ref, gate_ref = rest
    else:
        (gate_ref,) = rest
    x = h_ref[...]
    hn = _rms(x, g_ref[...]).astype(BF16)
    z = _dot(hn, win_ref[...])
    z = 0.5 * z * (1.0 + lax.erf(z * math.sqrt(0.5)))
    u = z[:, :D_MODEL]
    v = _rms(z[:, D_MODEL:], vg_ref[...])
    if emit_v:
        v_ref[...] = v
    vb = v.astype(BF16)
    row = lax.broadcasted_iota(jnp.int32, (chunk, chunk), 0)
    col = lax.broadcasted_iota(jnp.int32, (chunk, chunk), 1)
    lower = row >= col
    w = [jnp.where(lower, ws_ref[gi], 0.0).astype(BF16) for gi in range(A_GROUPS)]
    bias = bias_ref[...]
    for c in range(x.shape[0] // chunk):
        rows = slice(c * chunk, (c + 1) * chunk)
        for gi in range(A_GROUPS):
            cols = slice(gi * A_GROUP_DIM, (gi + 1) * A_GROUP_DIM)
            sv = _dot(w[gi], vb[rows, cols]) + bias[:, cols]
            gate_ref[rows, cols] = (u[rows, cols] * sv).astype(BF16)
    o_ref[...] = x + _dot(gate_ref[...], wout_ref[...])


def _gmlp(h, g, w_in, v_g, w_s, bias, w_out, chunk, emit_v):
    t = h.shape[0]
    tm = min(TOKEN_TILE, t)
    tok = pl.BlockSpec((tm, D_MODEL), lambda i: (i, 0))
    n_out = 2 if emit_v else 1
    return pl.pallas_call(
        functools.partial(_gmlp_kernel, chunk=chunk, emit_v=emit_v),
        grid=(t // tm,),
        in_specs=[tok, _const_spec((1, D_MODEL)), _const_spec(w_in.shape),
                  _const_spec((1, D_MODEL)), _const_spec(w_s.shape),
                  _const_spec(bias.shape), _const_spec(w_out.shape)],
        out_specs=[tok] * n_out,
        out_shape=[jax.ShapeDtypeStruct((t, D_MODEL), F32)] * n_out,
        scratch_shapes=[pltpu.VMEM((tm, D_MODEL), BF16)],
        compiler_params=_params(),
        name="gmlp",
    )(h, g, w_in, v_g, w_s, bias, w_out)


def _latent_kernel(h_ref, g_ref, wc_ref, wr_ref, wrr_ref, cg_ref, cos_ref, sin_ref,
                   ckv_ref, kr_ref, *kv_refs):
    hn = _rms(h_ref[...], g_ref[...]).astype(BF16)
    ckv = _rms(_dot(hn, wc_ref[...]), cg_ref[...])
    kr4 = _dot(hn, wr_ref[...]) * cos_ref[...] + _dot(hn, wrr_ref[...]) * sin_ref[...]
    ckv_ref[...] = ckv
    kr_ref[...] = kr4[:, :QK_ROPE]
    if kv_refs:
        keys_ref, vals_ref = kv_refs
        keys_ref[0, 0, :, :KV_LORA] = ckv.astype(BF16)
        keys_ref[0, 0, :, KV_LORA:] = kr4.astype(BF16)
        vals_ref[0, 0, :KV_LORA, :] = ckv.T.astype(BF16)
        vals_ref[0, 0, KV_LORA:, :] = jnp.ones((V_ROWS - KV_LORA, ATT_TK), BF16)


def _latent(h, g, w_c, w_r4, w_rr4, c_g, cos4, sin4, batch, emit_kv):
    t = h.shape[0]
    s = t // batch
    tm = min(ATT_TK, s)
    assert s % tm == 0 and (tm == ATT_TK or not emit_kv)
    n_pos = s // tm
    tok = lambda w: pl.BlockSpec((tm, w), lambda i: (i, 0))
    pos = pl.BlockSpec((tm, LANES), lambda i: (i % n_pos, 0))
    out_specs = [tok(KV_LORA), tok(QK_ROPE)]
    out_shape = [jax.ShapeDtypeStruct((t, KV_LORA), F32),
                 jax.ShapeDtypeStruct((t, QK_ROPE), F32)]
    if emit_kv:
        tile = lambda r, c: pl.BlockSpec((1, 1, r, c), lambda i: (i // n_pos, i % n_pos, 0, 0))
        out_specs += [tile(ATT_TK, QK_WIDTH), tile(V_ROWS, ATT_TK)]
        out_shape += [jax.ShapeDtypeStruct((batch, n_pos, ATT_TK, QK_WIDTH), BF16),
                      jax.ShapeDtypeStruct((batch, n_pos, V_ROWS, ATT_TK), BF16)]
    return pl.pallas_call(
        _latent_kernel,
        grid=(t // tm,),
        in_specs=[tok(D_MODEL), _const_spec((1, D_MODEL)), _const_spec(w_c.shape),
                  _const_spec(w_r4.shape), _const_spec(w_rr4.shape),
                  _const_spec((1, KV_LORA)), pos, pos],
        out_specs=out_specs,
        out_shape=out_shape,
        compiler_params=_params(),
        name="latent",
    )(h, g, w_c, w_r4, w_rr4, c_g, cos4, sin4)


def _qproj_kernel(h_ref, g_ref, wdq_ref, qg_ref, wn_ref, wk_ref, wr_ref, wrr_ref,
                  cos_ref, sin_ref, qt_ref):
    tm = h_ref.shape[0]
    cols = cos_ref.shape[1]
    hn = _rms(h_ref[...], g_ref[...]).astype(BF16)
    qc = _rms(_dot(hn, wdq_ref[...]), qg_ref[...])
    if tm < cols:
        qc = jnp.concatenate([qc, jnp.zeros((cols - tm, Q_LORA), F32)], axis=0)
    qc_t = qc.T.astype(BF16)
    qn_t = _dot(wn_ref[...], qc_t).astype(BF16)
    n_blocks = N_HEADS // HEADS_PER_ROPE_BLOCK
    cos = jnp.concatenate([cos_ref[...]] * n_blocks, axis=0)
    sin = jnp.concatenate([sin_ref[...]] * n_blocks, axis=0)
    qr_t = (_dot(wr_ref[...], qc_t) * cos + _dot(wrr_ref[...], qc_t) * sin) * Q_SCALE
    n_tiles = cols // ATT_TQ

    def put(hd, d0, x):
        for i in range(n_tiles):
            qt_ref[0, i, d0:d0 + LANES, hd * ATT_TQ:(hd + 1) * ATT_TQ] = (
                x[:, i * ATT_TQ:(i + 1) * ATT_TQ].astype(BF16))

    for p in range(N_HEADS // 2):
        qa_t = _dot(wk_ref[p], qn_t[p * LANES:(p + 1) * LANES]) * Q_SCALE
        put(2 * p, 0, qa_t[:KV_LORA])
        put(2 * p + 1, 0, qa_t[KV_LORA:])
    row_head = lax.broadcasted_iota(jnp.int32, (LANES, cols), 0) // QK_ROPE
    for hd in range(N_HEADS):
        blk = hd // HEADS_PER_ROPE_BLOCK
        own = row_head == hd % HEADS_PER_ROPE_BLOCK
        put(hd, KV_LORA, jnp.where(own, qr_t[blk * LANES:(blk + 1) * LANES], 0.0))


def _qproj(h, g, w_dq, q_g, w_n, w_k, w_r, w_rr, cos_t, sin_t, batch):
    t = h.shape[0]
    s = t // batch
    tm = min(TOKEN_TILE, s)
    n_s = s // tm
    cols = max(tm, ATT_TQ)
    n_tiles = cols // ATT_TQ
    tok = pl.BlockSpec((tm, D_MODEL), lambda b, i: (b * n_s + i, 0))
    pos = pl.BlockSpec((LANES, cols), lambda b, i: (0, i))
    return pl.pallas_call(
        _qproj_kernel,
        grid=(batch, n_s),
        in_specs=[tok, _const_spec((1, D_MODEL)), _const_spec(w_dq.shape),
                  _const_spec((1, Q_LORA)), _const_spec(w_n.shape), _const_spec(w_k.shape),
                  _const_spec(w_r.shape), _const_spec(w_rr.shape), pos, pos],
        out_specs=pl.BlockSpec((1, n_tiles, QK_WIDTH, N_HEADS * ATT_TQ),
                               lambda b, i: (b, i, 0, 0)),
        out_shape=jax.ShapeDtypeStruct(
            (batch, n_s * n_tiles, QK_WIDTH, N_HEADS * ATT_TQ), BF16),
        compiler_params=_params(2),
        name="qproj",
    )(h, g, w_dq, q_g, w_n, w_k, w_r, w_rr, cos_t, sin_t)


def _attn_kernel(qt_ref, qn_ref, k_ref, vt_ref, o_ref, s0_ref, s1_ref, s2_ref, s3_ref, smax_ref,
                 m_ref, acc_ref, *, causal, n_keys, prefetch):
    n_cols = N_HEADS * ATT_TQ
    step = pl.program_id(1)
    m_ref[...] = jnp.full(m_ref.shape, -jnp.inf, F32)
    acc_ref[...] = jnp.zeros(acc_ref.shape, F32)

    if causal:
        q_start = step * ATT_TQ
        n_full = q_start // ATT_TK
        q_pos = q_start + lax.broadcasted_iota(jnp.int32, (1, n_cols), 1) % ATT_TQ
        limit = (q_pos // CHUNK + 1) * CHUNK
    else:
        n_full = (n_keys - 1) // ATT_TK
        limit = n_keys

    s_refs = (s0_ref, s1_ref, s2_ref, s3_ref)
    LAST_BUF = 3

    def scores(kt, slot, q_ref=qt_ref):
        s = _dot(k_ref[0, kt], q_ref[0, 0])
        s_refs[slot][...] = s
        if slot != LAST_BUF:
            smax_ref[slot:slot + 1, :] = jnp.max(s, axis=0, keepdims=True)

    def softmax_pv(kt, slot, n_rows=None):
        s_ref = s_refs[slot]
        if n_rows is None:
            n_rows = ATT_TK
            smax = smax_ref[slot:slot + 1, :]
        else:
            k_pos = kt * ATT_TK + lax.broadcasted_iota(jnp.int32, (n_rows, 1), 0)
            s = jnp.where(k_pos < limit, s_ref[:n_rows, :], -jnp.inf)
            s_ref[:n_rows, :] = s
            smax = jnp.max(s, axis=0, keepdims=True)
        m_prev = m_ref[...]
        m_new = jnp.maximum(m_prev, smax)
        m_ref[...] = m_new
        p = jnp.exp2(s_ref[:n_rows, :] - m_new)
        alpha = jnp.exp2(m_prev - m_new)
        acc_ref[...] = acc_ref[...] * alpha + _dot(vt_ref[0, kt, :, :n_rows], p.astype(BF16))

    def finish(slot, n_rows):
        if prefetch and slot != 0:
            scores(0, 0, qn_ref)
        softmax_pv(n_full, slot, n_rows)
        if prefetch and slot == 0:
            scores(0, 0, qn_ref)

    def last_tile(slot):
        if causal:
            visible = q_start % ATT_TK + ATT_TQ
            r = (visible - 1) // ATT_LAST_STEP
            for k in range(ATT_TK // ATT_LAST_STEP):
                pl.when(r == k)(functools.partial(finish, slot, (k + 1) * ATT_LAST_STEP))
        else:
            tail = n_keys - n_full * ATT_TK
            finish(slot, -(-tail // LANES) * LANES)

    def triple_body(i, carry):
        kt = 3 * i
        scores(kt + 1, 1)
        scores(kt + 2, 2)
        softmax_pv(kt, 0)
        scores(kt + 3, 0)
        softmax_pv(kt + 1, 1)
        softmax_pv(kt + 2, 2)
        return carry

    def tail(first, n_left):
        for j in range(1, n_left):
            scores(first + j, j)
        for j in range(n_left):
            if j == min(1, n_left - 1):
                scores(n_full, LAST_BUF)
            softmax_pv(first + j, j)

    if prefetch:
        pl.when(step == 0)(functools.partial(scores, 0, 0))
    else:
        scores(0, 0)
    if causal:
        pl.when(n_full == 0)(functools.partial(finish, 0, ATT_TK))

        @pl.when(n_full > 0)
        def _():
            n_triples = (n_full - 1) // 3
            lax.fori_loop(0, n_triples, triple_body, 0)
            first = 3 * n_triples
            for n_left in (1, 2, 3):
                pl.when(n_full - first == n_left)(functools.partial(tail, first, n_left))
            last_tile(LAST_BUF)
    else:
        assert n_full > 0
        n_triples = (n_full - 1) // 3
        lax.fori_loop(0, n_triples, triple_body, 0)
        tail(3 * n_triples, n_full - 3 * n_triples)
        last_tile(LAST_BUF)

    o = acc_ref[:KV_LORA, :] / acc_ref[KV_LORA:KV_LORA + 1, :]
    rows_out = o_ref.shape[1]
    for hd in range(N_HEADS):
        o_hd = o[:, hd * ATT_TQ:(hd + 1) * ATT_TQ].T
        o_ref[0, :, hd * KV_LORA:(hd + 1) * KV_LORA] = o_hd[:rows_out].astype(BF16)


def _attention(qt, keys, vals_t, s, causal, n_keys):
    batch, n_q = qt.shape[:2]
    n_kt = keys.shape[1]
    rows_out = min(s, ATT_TQ)
    n_cols = N_HEADS * ATT_TQ
    q_spec = lambda nxt: pl.BlockSpec(
        (1, 1, QK_WIDTH, n_cols), lambda b, i: (b, jnp.minimum(i + nxt, n_q - 1), 0, 0))
    score_buf = pltpu.VMEM((ATT_TK, n_cols), F32)
    return pl.pallas_call(
        functools.partial(_attn_kernel, causal=causal, n_keys=n_keys, prefetch=n_q > 1),
        grid=(batch, n_q),
        in_specs=[q_spec(0), q_spec(1),
                  pl.BlockSpec((1, n_kt, ATT_TK, QK_WIDTH), lambda b, i: (b, 0, 0, 0)),
                  pl.BlockSpec((1, n_kt, V_ROWS, ATT_TK), lambda b, i: (b, 0, 0, 0))],
        out_specs=pl.BlockSpec((1, rows_out, N_HEADS * KV_LORA), lambda b, i: (b, i, 0)),
        out_shape=jax.ShapeDtypeStruct((batch, s, N_HEADS * KV_LORA), BF16),
        scratch_shapes=[score_buf, score_buf, score_buf, score_buf,
                        pltpu.VMEM((3, n_cols), F32), pltpu.VMEM((1, n_cols), F32),
                        pltpu.VMEM((V_ROWS, n_cols), F32)],
        compiler_params=pltpu.CompilerParams(
            dimension_semantics=("arbitrary", "arbitrary"), vmem_limit_bytes=VMEM_LIMIT),
        name="attention",
    )(qt, qt, keys, vals_t)


def _oproj_kernel(h_ref, a_ref, wv_ref, wo_ref, o_ref, cat_ref):
    for p in range(N_HEADS // 2):
        cat_ref[:, p * LANES:(p + 1) * LANES] = _dot(
            a_ref[:, p * 2 * KV_LORA:(p + 1) * 2 * KV_LORA], wv_ref[p]).astype(BF16)
    o_ref[...] = h_ref[...] + _dot(cat_ref[...], wo_ref[...])


def _oproj(h, a, w_v, w_o):
    t = h.shape[0]
    tm = min(TOKEN_TILE, t)
    tok = lambda w: pl.BlockSpec((tm, w), lambda i: (i, 0))
    return pl.pallas_call(
        _oproj_kernel,
        grid=(t // tm,),
        in_specs=[tok(D_MODEL), tok(N_HEADS * KV_LORA), _const_spec(w_v.shape),
                  _const_spec(w_o.shape)],
        out_specs=tok(D_MODEL),
        out_shape=jax.ShapeDtypeStruct((t, D_MODEL), F32),
        scratch_shapes=[pltpu.VMEM((tm, D_MODEL), BF16)],
        compiler_params=_params(),
        name="oproj",
    )(h, a, w_v, w_o)


def _block_diag_pairs(w):
    n2, r, c = w.shape
    w = w.reshape(n2 // 2, 2, r, c)
    z = jnp.zeros_like(w[:, 0])
    top = jnp.concatenate([w[:, 0], z], axis=2)
    bot = jnp.concatenate([z, w[:, 1]], axis=2)
    return jnp.concatenate([top, bot], axis=1)


def _rot_half_cols(w):
    shp = w.shape
    w = w.reshape(shp[0], -1, 2, QK_ROPE // 2)
    return jnp.stack([-w[:, :, 1], w[:, :, 0]], axis=2).reshape(shp)


def _rope_tables(pos):
    half = QK_ROPE // 2
    inv = 1.0 / (ROPE_BASE ** (jnp.arange(half, dtype=F32) * (2.0 / QK_ROPE)))
    ang = pos.astype(F32)[:, None] * inv[None, :]
    cos = jnp.cos(ang)
    sin = jnp.sin(ang)
    return jnp.concatenate([cos, cos], axis=1), jnp.concatenate([sin, sin], axis=1)


def _key_value_tiles(ckv, kr):
    b, k, _ = ckv.shape
    n = -(-k // ATT_TK)
    pad = ((0, 0), (0, n * ATT_TK - k), (0, 0))
    ckv = jnp.pad(ckv.astype(BF16), pad)
    kr = jnp.pad(kr.astype(BF16), pad)
    keys = jnp.concatenate([ckv] + [kr] * HEADS_PER_ROPE_BLOCK, axis=2)
    keys = keys.reshape(b, n, ATT_TK, QK_WIDTH)
    vals_t = ckv.reshape(b, n, ATT_TK, KV_LORA).transpose(0, 1, 3, 2)
    ones = jnp.ones((b, n, V_ROWS - KV_LORA, ATT_TK), BF16)
    return keys, jnp.concatenate([vals_t, ones], axis=2)


def kernel(x_prompt, x_sample, cache_ckv, cache_krope, ffn1_norm, ffn1_w_gu, ffn1_w_down, mix_norm, ffn2_norm, ffn2_w_gu, ffn2_w_down, a_w_in, a_v_norm, a_w_s, a_b_s, a_w_out, kv_norm, w_dkv, ckv_norm, w_uk, w_uv, b_w_dq, b_q_norm, b_w_uq, b_w_o, final_norm):
    n_b = DEPTH - N_A
    row = lambda v: v.reshape(1, -1)

    ffn1 = (ffn1_w_gu.astype(BF16), ffn1_w_down.astype(BF16))
    ffn2 = (ffn2_w_gu.astype(BF16), ffn2_w_down.astype(BF16))
    a_in = a_w_in.astype(BF16)
    a_out = a_w_out.astype(BF16)

    w_c = w_dkv[:, :KV_LORA].astype(BF16)
    w_kr = w_dkv[:, KV_LORA:]
    w_kr4 = jnp.tile(w_kr, (1, HEADS_PER_ROPE_BLOCK)).astype(BF16)
    w_kr_rot4 = jnp.tile(_rot_half_cols(w_kr), (1, HEADS_PER_ROPE_BLOCK)).astype(BF16)

    w_uq = b_w_uq.reshape(n_b, Q_LORA, N_HEADS, QK_NOPE + QK_ROPE)
    w_qn = w_uq[..., :QK_NOPE].reshape(n_b, Q_LORA, N_HEADS * QK_NOPE)
    w_qn_t = w_qn.transpose(0, 2, 1).astype(BF16)
    w_qr = w_uq[..., QK_NOPE:].reshape(n_b, Q_LORA, N_HEADS * QK_ROPE)
    w_qr_rot_t = jnp.stack([_rot_half_cols(w_qr[j]).T for j in range(n_b)]).astype(BF16)
    w_qr_t = w_qr.transpose(0, 2, 1).astype(BF16)
    w_dq = b_w_dq.astype(BF16)
    uk_heads = w_uk.reshape(KV_LORA, N_HEADS, QK_NOPE).transpose(1, 2, 0)
    w_absorb_t = _block_diag_pairs(uk_heads).transpose(0, 2, 1).astype(BF16)
    uv_heads = w_uv.reshape(KV_LORA, N_HEADS, V_DIM).transpose(1, 0, 2)
    w_expand = _block_diag_pairs(uv_heads).astype(BF16)
    w_o = b_w_o.astype(BF16)

    def run(x, pos, chunk, past_ckv, past_krope):
        batch, s, _ = x.shape
        causal = past_ckv is None
        h = x.reshape(batch * s, D_MODEL)
        cos, sin = _rope_tables(pos)
        cos4 = jnp.concatenate([cos] * HEADS_PER_ROPE_BLOCK, axis=1)
        sin4 = jnp.concatenate([sin] * HEADS_PER_ROPE_BLOCK, axis=1)
        lane_pad = ((0, 0), (0, max(ATT_TQ - s, 0)))
        cos_t = jnp.pad(cos4.T, lane_pad)
        sin_t = jnp.pad(sin4.T, lane_pad)
        v_rows = []
        ckv = kr = keys = vals_t = None
        n_keys = s
        for l in range(DEPTH):
            h = _ffn(h, row(ffn1_norm[l]), *ffn1, l)
            if l < N_A:
                bias = jnp.repeat(a_b_s[l][:, :chunk].T, A_GROUP_DIM, axis=1)
                h, *v = _gmlp(h, row(mix_norm[l]), a_in[l], row(a_v_norm[l]),
                              a_w_s[l][:, :chunk, :chunk], bias, a_out[l], chunk,
                              emit_v=not causal)
                v_rows.extend(vi.reshape(batch, s, D_MODEL) for vi in v)
            else:
                j = l - N_A
                qt = _qproj(h, row(mix_norm[l]), w_dq[j], row(b_q_norm[j]), w_qn_t[j],
                            w_absorb_t, w_qr_t[j], w_qr_rot_t[j], cos_t, sin_t, batch)
                a = _attention(qt, keys, vals_t, s, causal, n_keys)
                h = _oproj(h, a.reshape(batch * s, N_HEADS * KV_LORA), w_expand, w_o[j])
            h = _ffn(h, row(ffn2_norm[l]), *ffn2, l,
                     final_g=row(final_norm) if l == DEPTH - 1 else None)
            if l == N_A - 1:
                ckv, kr, *kv = _latent(h, row(kv_norm), w_c, w_kr4, w_kr_rot4, row(ckv_norm),
                                       cos4, sin4, batch, emit_kv=causal)
                ckv = ckv.reshape(batch, s, KV_LORA)
                kr = kr.reshape(batch, s, QK_ROPE)
                if causal:
                    keys, vals_t = kv
                else:
                    ckv_all = jnp.concatenate([past_ckv, ckv], axis=1)
                    kr_all = jnp.concatenate([past_krope, kr], axis=1)
                    n_keys = ckv_all.shape[1]
                    keys, vals_t = _key_value_tiles(ckv_all, kr_all)
        return h.reshape(batch, s, D_MODEL), ckv, kr, v_rows

    s_p = x_prompt.shape[1]
    y_p, ckv_p, kr_p, _ = run(x_prompt, jnp.arange(s_p), A_CHUNK, None, None)
    past = cache_ckv.shape[1]
    s_s = x_sample.shape[1]
    y_s, ckv_s, kr_s, a_rows = run(x_sample, past + jnp.arange(s_s), s_s, cache_ckv, cache_krope)
    return (y_p, y_s, ckv_p, kr_p, ckv_s, kr_s, jnp.stack(a_rows, axis=0))
```

```python
import functools
import math

import jax
import jax.numpy as jnp
from jax import lax
from jax.experimental import pallas as pl
from jax.experimental.pallas import tpu as pltpu

D_MODEL = 1024
DEPTH = 4
N_A = DEPTH // 2
CHUNK = 64
A_CHUNK = 128
A_GROUPS = 8
A_GROUP_DIM = D_MODEL // A_GROUPS
N_HEADS = 16
QK_NOPE = 64
QK_ROPE = 32
V_DIM = 64
KV_LORA = 128
Q_LORA = 256
D_FF = 2816
ROPE_BASE = 10000.0
EPS = 1e-6
ATT_SCALE = (QK_NOPE + QK_ROPE) ** -0.5
Q_SCALE = ATT_SCALE * math.log2(math.e)

LANES = 128
MXU_DIM = 256
VMEM_LIMIT = 56 * 1024 * 1024

TOKEN_TILE = 1024
FF_CHUNK = MXU_DIM
HEADS_PER_ROPE_BLOCK = LANES // QK_ROPE
QK_WIDTH = KV_LORA + LANES
ATT_TQ = LANES
ATT_TK = 2 * MXU_DIM
ATT_LAST_STEP = LANES
BF16_SUBLANES = 16
V_ROWS = KV_LORA + BF16_SUBLANES

BF16 = jnp.bfloat16
F32 = jnp.float32


def _dot(a, b):
    return jnp.dot(a, b, preferred_element_type=F32)


def _rms(x, g):
    return x * lax.rsqrt(jnp.mean(x * x, axis=-1, keepdims=True) + EPS) * g


def _const_spec(shape):
    nd = len(shape)
    return pl.BlockSpec(shape, lambda *_: (0,) * nd, pipeline_mode=pl.Buffered(1))


def _token_tile(n_rows):
    tm = min(TOKEN_TILE, n_rows)
    assert n_rows % tm == 0, (n_rows, tm)
    return tm


def _params(n_axes=1):
    return pltpu.CompilerParams(
        dimension_semantics=("parallel",) * n_axes, vmem_limit_bytes=VMEM_LIMIT)


def _ffn_kernel(h_ref, g_ref, wgu_ref, wd_ref, *rest, final_norm):
    if final_norm:
        fg_ref, o_ref = rest
    else:
        (o_ref,) = rest
    x = h_ref[...]
    xn = _rms(x, g_ref[...]).astype(BF16)
    acc = jnp.zeros(x.shape, F32)
    for lo in range(0, D_FF, FF_CHUNK):
        hi = min(lo + FF_CHUNK, D_FF)
        g = _dot(xn, wgu_ref[:, lo:hi])
        u = _dot(xn, wgu_ref[:, D_FF + lo:D_FF + hi])
        a = (g * jax.nn.sigmoid(g) * u).astype(BF16)
        acc = acc + _dot(a, wd_ref[lo:hi, :])
    y = x + 0.5 * acc
    if final_norm:
        y = _rms(y, fg_ref[...])
    o_ref[...] = y


def _layer_spec(shape, layer):
    nd = len(shape) - 1
    return pl.BlockSpec((None,) + tuple(shape[1:]), lambda *_: (layer,) + (0,) * nd,
                        pipeline_mode=pl.Buffered(1))


def _ffn(h, g, wgu, wd, layer, final_g=None):
    t = h.shape[0]
    tm = _token_tile(t)
    tok = pl.BlockSpec((tm, D_MODEL), lambda i: (i, 0))
    in_specs = [tok, _const_spec((1, D_MODEL)), _layer_spec(wgu.shape, layer),
                _layer_spec(wd.shape, layer)]
    args = [h, g, wgu, wd]
    if final_g is not None:
        in_specs.append(_const_spec((1, D_MODEL)))
        args.append(final_g)
    return pl.pallas_call(
        functools.partial(_ffn_kernel, final_norm=final_g is not None),
        grid=(t // tm,),
        in_specs=in_specs,
        out_specs=tok,
        out_shape=jax.ShapeDtypeStruct((t, D_MODEL), F32),
        compiler_params=_params(),
        name="ffn",
    )(*args)


def _gmlp_kernel(h_ref, g_ref, win_ref, vg_ref, ws_ref, bias_ref, wout_ref,
                 o_ref, *rest, chunk, emit_v):
    if emit_v:
        v_ref, gate_ref = rest
    else:
        (gate_ref,) = rest
    x = h_ref[...]
    hn = _rms(x, g_ref[...]).astype(BF16)
    z = _dot(hn, win_ref[...])
    z = 0.5 * z * (1.0 + lax.erf(z * math.sqrt(0.5)))
    u = z[:, :D_MODEL]
    v = _rms(z[:, D_MODEL:], vg_ref[...])
    if emit_v:
        v_ref[...] = v
    vb = v.astype(BF16)
    row = lax.broadcasted_iota(jnp.int32, (chunk, chunk), 0)
    col = lax.broadcasted_iota(jnp.int32, (chunk, chunk), 1)
    lower = row >= col
    w = [jnp.where(lower, ws_ref[gi], 0.0).astype(BF16) for gi in range(A_GROUPS)]
    bias = bias_ref[...]
    for c in range(x.shape[0] // chunk):
        rows = slice(c * chunk, (c + 1) * chunk)
        for gi in range(A_GROUPS):
            cols = slice(gi * A_GROUP_DIM, (gi + 1) * A_GROUP_DIM)
            sv = _dot(w[gi], vb[rows, cols]) + bias[:, cols]
            gate_ref[rows, cols] = (u[rows, cols] * sv).astype(BF16)
    o_ref[...] = x + _dot(gate_ref[...], wout_ref[...])


def _gmlp(h, g, w_in, v_g, w_s, bias, w_out, chunk, emit_v):
    t = h.shape[0]
    tm = _token_tile(t)
    assert tm % chunk == 0, (tm, chunk)
    tok = pl.BlockSpec((tm, D_MODEL), lambda i: (i, 0))
    n_out = 2 if emit_v else 1
    return pl.pallas_call(
        functools.partial(_gmlp_kernel, chunk=chunk, emit_v=emit_v),
        grid=(t // tm,),
        in_specs=[tok, _const_spec((1, D_MODEL)), _const_spec(w_in.shape),
                  _const_spec((1, D_MODEL)), _const_spec(w_s.shape),
                  _const_spec(bias.shape), _const_spec(w_out.shape)],
        out_specs=[tok] * n_out,
        out_shape=[jax.ShapeDtypeStruct((t, D_MODEL), F32)] * n_out,
        scratch_shapes=[pltpu.VMEM((tm, D_MODEL), BF16)],
        compiler_params=_params(),
        name="gmlp",
    )(h, g, w_in, v_g, w_s, bias, w_out)


def _latent_kernel(h_ref, g_ref, wc_ref, wr_ref, wrr_ref, cg_ref, cos_ref, sin_ref,
                   ckv_ref, kr_ref, *kv_refs):
    hn = _rms(h_ref[...], g_ref[...]).astype(BF16)
    ckv = _rms(_dot(hn, wc_ref[...]), cg_ref[...])
    kr4 = _dot(hn, wr_ref[...]) * cos_ref[...] + _dot(hn, wrr_ref[...]) * sin_ref[...]
    ckv_ref[...] = ckv
    kr_ref[...] = kr4[:, :QK_ROPE]
    if kv_refs:
        keys_ref, vals_ref = kv_refs
        keys_ref[0, 0, :, :KV_LORA] = ckv.astype(BF16)
        keys_ref[0, 0, :, KV_LORA:] = kr4.astype(BF16)
        vals_ref[0, 0, :KV_LORA, :] = ckv.T.astype(BF16)
        vals_ref[0, 0, KV_LORA:, :] = jnp.ones((V_ROWS - KV_LORA, ATT_TK), BF16)


def _latent(h, g, w_c, w_r4, w_rr4, c_g, cos4, sin4, batch, emit_kv):
    t = h.shape[0]
    s = t // batch
    tm = min(ATT_TK, s)
    assert s % tm == 0 and (tm == ATT_TK or not emit_kv)
    n_pos = s // tm
    tok = lambda w: pl.BlockSpec((tm, w), lambda i: (i, 0))
    pos = pl.BlockSpec((tm, LANES), lambda i: (i % n_pos, 0))
    out_specs = [tok(KV_LORA), tok(QK_ROPE)]
    out_shape = [jax.ShapeDtypeStruct((t, KV_LORA), F32),
                 jax.ShapeDtypeStruct((t, QK_ROPE), F32)]
    if emit_kv:
        tile = lambda r, c: pl.BlockSpec((1, 1, r, c), lambda i: (i // n_pos, i % n_pos, 0, 0))
        out_specs += [tile(ATT_TK, QK_WIDTH), tile(V_ROWS, ATT_TK)]
        out_shape += [jax.ShapeDtypeStruct((batch, n_pos, ATT_TK, QK_WIDTH), BF16),
                      jax.ShapeDtypeStruct((batch, n_pos, V_ROWS, ATT_TK), BF16)]
    return pl.pallas_call(
        _latent_kernel,
        grid=(t // tm,),
        in_specs=[tok(D_MODEL), _const_spec((1, D_MODEL)), _const_spec(w_c.shape),
                  _const_spec(w_r4.shape), _const_spec(w_rr4.shape),
                  _const_spec((1, KV_LORA)), pos, pos],
        out_specs=out_specs,
        out_shape=out_shape,
        compiler_params=_params(),
        name="latent",
    )(h, g, w_c, w_r4, w_rr4, c_g, cos4, sin4)


def _qproj_kernel(h_ref, g_ref, wdq_ref, qg_ref, wn_ref, wk_ref, wr_ref, wrr_ref,
                  cos_ref, sin_ref, qt_ref):
    tm = h_ref.shape[0]
    cols = cos_ref.shape[1]
    hn = _rms(h_ref[...], g_ref[...]).astype(BF16)
    qc = _rms(_dot(hn, wdq_ref[...]), qg_ref[...])
    if tm < cols:
        qc = jnp.concatenate([qc, jnp.zeros((cols - tm, Q_LORA), F32)], axis=0)
    qc_t = qc.T.astype(BF16)
    qn_t = _dot(wn_ref[...], qc_t).astype(BF16)
    n_blocks = N_HEADS // HEADS_PER_ROPE_BLOCK
    cos = jnp.concatenate([cos_ref[...]] * n_blocks, axis=0)
    sin = jnp.concatenate([sin_ref[...]] * n_blocks, axis=0)
    qr_t = (_dot(wr_ref[...], qc_t) * cos + _dot(wrr_ref[...], qc_t) * sin) * Q_SCALE
    n_tiles = cols // ATT_TQ

    def put(hd, d0, x):
        for i in range(n_tiles):
            qt_ref[0, i, d0:d0 + LANES, hd * ATT_TQ:(hd + 1) * ATT_TQ] = (
                x[:, i * ATT_TQ:(i + 1) * ATT_TQ].astype(BF16))

    for p in range(N_HEADS // 2):
        qa_t = _dot(wk_ref[p], qn_t[p * LANES:(p + 1) * LANES]) * Q_SCALE
        put(2 * p, 0, qa_t[:KV_LORA])
        put(2 * p + 1, 0, qa_t[KV_LORA:])
    row_head = lax.broadcasted_iota(jnp.int32, (LANES, cols), 0) // QK_ROPE
    for hd in range(N_HEADS):
        blk = hd // HEADS_PER_ROPE_BLOCK
        own = row_head == hd % HEADS_PER_ROPE_BLOCK
        put(hd, KV_LORA, jnp.where(own, qr_t[blk * LANES:(blk + 1) * LANES], 0.0))


def _qproj(h, g, w_dq, q_g, w_n, w_k, w_r, w_rr, cos_t, sin_t, batch):
    t = h.shape[0]
    s = t // batch
    tm = _token_tile(s)
    n_s = s // tm
    cols = max(tm, ATT_TQ)
    n_tiles = cols // ATT_TQ
    tok = pl.BlockSpec((tm, D_MODEL), lambda b, i: (b * n_s + i, 0))
    pos = pl.BlockSpec((LANES, cols), lambda b, i: (0, i))
    return pl.pallas_call(
        _qproj_kernel,
        grid=(batch, n_s),
        in_specs=[tok, _const_spec((1, D_MODEL)), _const_spec(w_dq.shape),
                  _const_spec((1, Q_LORA)), _const_spec(w_n.shape), _const_spec(w_k.shape),
                  _const_spec(w_r.shape), _const_spec(w_rr.shape), pos, pos],
        out_specs=pl.BlockSpec((1, n_tiles, QK_WIDTH, N_HEADS * ATT_TQ),
                               lambda b, i: (b, i, 0, 0)),
        out_shape=jax.ShapeDtypeStruct(
            (batch, n_s * n_tiles, QK_WIDTH, N_HEADS * ATT_TQ), BF16),
        compiler_params=_params(2),
        name="qproj",
    )(h, g, w_dq, q_g, w_n, w_k, w_r, w_rr, cos_t, sin_t)


def _attn_kernel(qt_ref, qn_ref, k_ref, vt_ref, o_ref, s0_ref, s1_ref, s2_ref, s3_ref, smax_ref,
                 m_ref, acc_ref, *, causal, n_keys, prefetch):
    n_cols = N_HEADS * ATT_TQ
    step = pl.program_id(1)
    m_ref[...] = jnp.full(m_ref.shape, -jnp.inf, F32)
    acc_ref[...] = jnp.zeros(acc_ref.shape, F32)

    if causal:
        q_start = step * ATT_TQ
        n_full = q_start // ATT_TK
        q_pos = q_start + lax.broadcasted_iota(jnp.int32, (1, n_cols), 1) % ATT_TQ
        limit = (q_pos // CHUNK + 1) * CHUNK
    else:
        n_full = (n_keys - 1) // ATT_TK
        limit = n_keys

    s_refs = (s0_ref, s1_ref, s2_ref, s3_ref)
    LAST_BUF = 3

    def scores(kt, slot, q_ref=qt_ref):
        s = _dot(k_ref[0, kt], q_ref[0, 0])
        s_refs[slot][...] = s
        if slot != LAST_BUF:
            smax_ref[slot:slot + 1, :] = jnp.max(s, axis=0, keepdims=True)

    def softmax_pv(kt, slot, n_rows=None):
        s_ref = s_refs[slot]
        if n_rows is None:
            n_rows = ATT_TK
            smax = smax_ref[slot:slot + 1, :]
        else:
            k_pos = kt * ATT_TK + lax.broadcasted_iota(jnp.int32, (n_rows, 1), 0)
            s = jnp.where(k_pos < limit, s_ref[:n_rows, :], -jnp.inf)
            s_ref[:n_rows, :] = s
            smax = jnp.max(s, axis=0, keepdims=True)
        m_prev = m_ref[...]
        m_new = jnp.maximum(m_prev, smax)
        m_ref[...] = m_new
        p = jnp.exp2(s_ref[:n_rows, :] - m_new)
        alpha = jnp.exp2(m_prev - m_new)
        acc_ref[...] = acc_ref[...] * alpha + _dot(vt_ref[0, kt, :, :n_rows], p.astype(BF16))

    def finish(slot, n_rows):
        if prefetch and slot != 0:
            scores(0, 0, qn_ref)
        softmax_pv(n_full, slot, n_rows)
        if prefetch and slot == 0:
            scores(0, 0, qn_ref)

    def last_tile(slot):
        if causal:
            visible = q_start % ATT_TK + ATT_TQ
            r = (visible - 1) // ATT_LAST_STEP
            for k in range(ATT_TK // ATT_LAST_STEP):
                pl.when(r == k)(functools.partial(finish, slot, (k + 1) * ATT_LAST_STEP))
        else:
            tail = n_keys - n_full * ATT_TK
            finish(slot, -(-tail // LANES) * LANES)

    def triple_body(i, carry):
        kt = 3 * i
        scores(kt + 1, 1)
        scores(kt + 2, 2)
        softmax_pv(kt, 0)
        scores(kt + 3, 0)
        softmax_pv(kt + 1, 1)
        softmax_pv(kt + 2, 2)
        return carry

    def tail(first, n_left):
        for j in range(1, n_left):
            scores(first + j, j)
        for j in range(n_left):
            if j == min(1, n_left - 1):
                scores(n_full, LAST_BUF)
            softmax_pv(first + j, j)

    if prefetch:
        pl.when(step == 0)(functools.partial(scores, 0, 0))
    else:
        scores(0, 0)
    if causal:
        pl.when(n_full == 0)(functools.partial(finish, 0, ATT_TK))

        @pl.when(n_full > 0)
        def _():
            n_triples = (n_full - 1) // 3
            lax.fori_loop(0, n_triples, triple_body, 0)
            first = 3 * n_triples
            for n_left in (1, 2, 3):
                pl.when(n_full - first == n_left)(functools.partial(tail, first, n_left))
            last_tile(LAST_BUF)
    else:
        assert n_full > 0
        n_triples = (n_full - 1) // 3
        lax.fori_loop(0, n_triples, triple_body, 0)
        tail(3 * n_triples, n_full - 3 * n_triples)
        last_tile(LAST_BUF)

    o = acc_ref[:KV_LORA, :] / acc_ref[KV_LORA:KV_LORA + 1, :]
    rows_out = o_ref.shape[1]
    for hd in range(N_HEADS):
        o_hd = o[:, hd * ATT_TQ:(hd + 1) * ATT_TQ].T
        o_ref[0, :, hd * KV_LORA:(hd + 1) * KV_LORA] = o_hd[:rows_out].astype(BF16)


def _attention(qt, keys, vals_t, s, causal, n_keys, allow_prefetch=True):
    batch, n_q = qt.shape[:2]
    n_kt = keys.shape[1]
    assert ATT_TK % ATT_TQ == 0 and (s % ATT_TQ == 0 or s < ATT_TQ)
    assert n_kt * ATT_TK >= (s if causal else n_keys)
    rows_out = min(s, ATT_TQ)
    n_cols = N_HEADS * ATT_TQ
    q_spec = lambda nxt: pl.BlockSpec(
        (1, 1, QK_WIDTH, n_cols), lambda b, i: (b, jnp.minimum(i + nxt, n_q - 1), 0, 0))
    score_buf = pltpu.VMEM((ATT_TK, n_cols), F32)
    return pl.pallas_call(
        functools.partial(_attn_kernel, causal=causal, n_keys=n_keys,
                          prefetch=allow_prefetch and n_q > 1),
        grid=(batch, n_q),
        in_specs=[q_spec(0), q_spec(1),
                  pl.BlockSpec((1, n_kt, ATT_TK, QK_WIDTH), lambda b, i: (b, 0, 0, 0)),
                  pl.BlockSpec((1, n_kt, V_ROWS, ATT_TK), lambda b, i: (b, 0, 0, 0))],
        out_specs=pl.BlockSpec((1, rows_out, N_HEADS * KV_LORA), lambda b, i: (b, i, 0)),
        out_shape=jax.ShapeDtypeStruct((batch, s, N_HEADS * KV_LORA), BF16),
        scratch_shapes=[score_buf, score_buf, score_buf, score_buf,
                        pltpu.VMEM((3, n_cols), F32), pltpu.VMEM((1, n_cols), F32),
                        pltpu.VMEM((V_ROWS, n_cols), F32)],
        compiler_params=pltpu.CompilerParams(
            dimension_semantics=("arbitrary", "arbitrary"), vmem_limit_bytes=VMEM_LIMIT),
        name="attention",
    )(qt, qt, keys, vals_t)


def _oproj_kernel(h_ref, a_ref, wv_ref, wo_ref, o_ref, cat_ref):
    for p in range(N_HEADS // 2):
        cat_ref[:, p * LANES:(p + 1) * LANES] = _dot(
            a_ref[:, p * 2 * KV_LORA:(p + 1) * 2 * KV_LORA], wv_ref[p]).astype(BF16)
    o_ref[...] = h_ref[...] + _dot(cat_ref[...], wo_ref[...])


def _oproj(h, a, w_v, w_o):
    t = h.shape[0]
    tm = _token_tile(t)
    tok = lambda w: pl.BlockSpec((tm, w), lambda i: (i, 0))
    return pl.pallas_call(
        _oproj_kernel,
        grid=(t // tm,),
        in_specs=[tok(D_MODEL), tok(N_HEADS * KV_LORA), _const_spec(w_v.shape),
                  _const_spec(w_o.shape)],
        out_specs=tok(D_MODEL),
        out_shape=jax.ShapeDtypeStruct((t, D_MODEL), F32),
        scratch_shapes=[pltpu.VMEM((tm, D_MODEL), BF16)],
        compiler_params=_params(),
        name="oproj",
    )(h, a, w_v, w_o)


def _block_diag_pairs(w):
    n2, r, c = w.shape
    w = w.reshape(n2 // 2, 2, r, c)
    z = jnp.zeros_like(w[:, 0])
    top = jnp.concatenate([w[:, 0], z], axis=2)
    bot = jnp.concatenate([z, w[:, 1]], axis=2)
    return jnp.concatenate([top, bot], axis=1)


def _rot_half_cols(w):
    shp = w.shape
    w = w.reshape(shp[0], -1, 2, QK_ROPE // 2)
    return jnp.stack([-w[:, :, 1], w[:, :, 0]], axis=2).reshape(shp)


def _rope_tables(pos):
    half = QK_ROPE // 2
    inv = 1.0 / (ROPE_BASE ** (jnp.arange(half, dtype=F32) * (2.0 / QK_ROPE)))
    ang = pos.astype(F32)[:, None] * inv[None, :]
    cos = jnp.cos(ang)
    sin = jnp.sin(ang)
    return jnp.concatenate([cos, cos], axis=1), jnp.concatenate([sin, sin], axis=1)


def _key_value_tiles(ckv, kr):
    b, k, _ = ckv.shape
    n = -(-k // ATT_TK)
    pad = ((0, 0), (0, n * ATT_TK - k), (0, 0))
    ckv = jnp.pad(ckv.astype(BF16), pad)
    kr = jnp.pad(kr.astype(BF16), pad)
    keys = jnp.concatenate([ckv] + [kr] * HEADS_PER_ROPE_BLOCK, axis=2)
    keys = keys.reshape(b, n, ATT_TK, QK_WIDTH)
    vals_t = ckv.reshape(b, n, ATT_TK, KV_LORA).transpose(0, 1, 3, 2)
    ones = jnp.ones((b, n, V_ROWS - KV_LORA, ATT_TK), BF16)
    return keys, jnp.concatenate([vals_t, ones], axis=2)


def kernel(x_prompt, x_sample, cache_ckv, cache_krope, ffn1_norm, ffn1_w_gu, ffn1_w_down, mix_norm, ffn2_norm, ffn2_w_gu, ffn2_w_down, a_w_in, a_v_norm, a_w_s, a_b_s, a_w_out, kv_norm, w_dkv, ckv_norm, w_uk, w_uv, b_w_dq, b_q_norm, b_w_uq, b_w_o, final_norm):
    n_b = DEPTH - N_A
    row = lambda v: v.reshape(1, -1)

    ffn1 = (ffn1_w_gu.astype(BF16), ffn1_w_down.astype(BF16))
    ffn2 = (ffn2_w_gu.astype(BF16), ffn2_w_down.astype(BF16))
    a_in = a_w_in.astype(BF16)
    a_out = a_w_out.astype(BF16)

    w_c = w_dkv[:, :KV_LORA].astype(BF16)
    w_kr = w_dkv[:, KV_LORA:]
    w_kr4 = jnp.tile(w_kr, (1, HEADS_PER_ROPE_BLOCK)).astype(BF16)
    w_kr_rot4 = jnp.tile(_rot_half_cols(w_kr), (1, HEADS_PER_ROPE_BLOCK)).astype(BF16)

    w_uq = b_w_uq.reshape(n_b, Q_LORA, N_HEADS, QK_NOPE + QK_ROPE)
    w_qn = w_uq[..., :QK_NOPE].reshape(n_b, Q_LORA, N_HEADS * QK_NOPE)
    w_qn_t = w_qn.transpose(0, 2, 1).astype(BF16)
    w_qr = w_uq[..., QK_NOPE:].reshape(n_b, Q_LORA, N_HEADS * QK_ROPE)
    w_qr_rot_t = jnp.stack([_rot_half_cols(w_qr[j]).T for j in range(n_b)]).astype(BF16)
    w_qr_t = w_qr.transpose(0, 2, 1).astype(BF16)
    w_dq = b_w_dq.astype(BF16)
    uk_heads = w_uk.reshape(KV_LORA, N_HEADS, QK_NOPE).transpose(1, 2, 0)
    w_absorb_t = _block_diag_pairs(uk_heads).transpose(0, 2, 1).astype(BF16)
    uv_heads = w_uv.reshape(KV_LORA, N_HEADS, V_DIM).transpose(1, 0, 2)
    w_expand = _block_diag_pairs(uv_heads).astype(BF16)
    w_o = b_w_o.astype(BF16)

    def run(x, pos, chunk, past_ckv, past_krope):
        batch, s, _ = x.shape
        causal = past_ckv is None
        h = x.reshape(batch * s, D_MODEL)
        cos, sin = _rope_tables(pos)
        cos4 = jnp.concatenate([cos] * HEADS_PER_ROPE_BLOCK, axis=1)
        sin4 = jnp.concatenate([sin] * HEADS_PER_ROPE_BLOCK, axis=1)
        lane_pad = ((0, 0), (0, max(ATT_TQ - s, 0)))
        cos_t = jnp.pad(cos4.T, lane_pad)
        sin_t = jnp.pad(sin4.T, lane_pad)
        v_rows = []
        ckv = kr = keys = vals_t = None
        n_keys = s
        for l in range(DEPTH):
            h = _ffn(h, row(ffn1_norm[l]), *ffn1, l)
            if l < N_A:
                bias = jnp.repeat(a_b_s[l][:, :chunk].T, A_GROUP_DIM, axis=1)
                h, *v = _gmlp(h, row(mix_norm[l]), a_in[l], row(a_v_norm[l]),
                              a_w_s[l][:, :chunk, :chunk], bias, a_out[l], chunk,
                              emit_v=not causal)
                v_rows.extend(vi.reshape(batch, s, D_MODEL) for vi in v)
            else:
                j = l - N_A
                qt = _qproj(h, row(mix_norm[l]), w_dq[j], row(b_q_norm[j]), w_qn_t[j],
                            w_absorb_t, w_qr_t[j], w_qr_rot_t[j], cos_t, sin_t, batch)
                a = _attention(qt, keys, vals_t, s, causal, n_keys, allow_prefetch=j == 0)
                h = _oproj(h, a.reshape(batch * s, N_HEADS * KV_LORA), w_expand, w_o[j])
            h = _ffn(h, row(ffn2_norm[l]), *ffn2, l,
                     final_g=row(final_norm) if l == DEPTH - 1 else None)
            if l == N_A - 1:
                ckv, kr, *kv = _latent(h, row(kv_norm), w_c, w_kr4, w_kr_rot4, row(ckv_norm),
                                       cos4, sin4, batch, emit_kv=causal)
                ckv = ckv.reshape(batch, s, KV_LORA)
                kr = kr.reshape(batch, s, QK_ROPE)
                if causal:
                    keys, vals_t = kv
                else:
                    ckv_all = jnp.concatenate([past_ckv, ckv], axis=1)
                    kr_all = jnp.concatenate([past_krope, kr], axis=1)
                    n_keys = ckv_all.shape[1]
                    keys, vals_t = _key_value_tiles(ckv_all, kr_all)
        return h.reshape(batch, s, D_MODEL), ckv, kr, v_rows

    s_p = x_prompt.shape[1]
    y_p, ckv_p, kr_p, _ = run(x_prompt, jnp.arange(s_p), A_CHUNK, None, None)
    past = cache_ckv.shape[1]
    s_s = x_sample.shape[1]
    y_s, ckv_s, kr_s, a_rows = run(x_sample, past + jnp.arange(s_s), s_s, cache_ckv, cache_krope)
    return (y_p, y_s, ckv_p, kr_p, ckv_s, kr_s, jnp.stack(a_rows, axis=0))
```

```python
import functools
import math

import jax
import jax.numpy as jnp
from jax import lax
from jax.experimental import pallas as pl
from jax.experimental.pallas import tpu as pltpu

D_MODEL = 1024
DEPTH = 4
N_A = DEPTH // 2
CHUNK = 64
A_CHUNK = 128
A_GROUPS = 8
A_GROUP_DIM = D_MODEL // A_GROUPS
N_HEADS = 16
QK_NOPE = 64
QK_ROPE = 32
V_DIM = 64
KV_LORA = 128
Q_LORA = 256
D_FF = 2816
ROPE_BASE = 10000.0
EPS = 1e-6
ATT_SCALE = (QK_NOPE + QK_ROPE) ** -0.5
Q_SCALE = ATT_SCALE * math.log2(math.e)

LANES = 128
MXU_DIM = 256
VMEM_LIMIT = 56 * 1024 * 1024

TOKEN_TILE = 1024
FUSED_TOKEN_TILE = 512
FF_CHUNK = MXU_DIM
HEADS_PER_ROPE_BLOCK = LANES // QK_ROPE
QK_WIDTH = KV_LORA + LANES
ATT_TQ = LANES
ATT_TK = 2 * MXU_DIM
ATT_LAST_STEP = LANES
BF16_SUBLANES = 16
V_ROWS = KV_LORA + BF16_SUBLANES

BF16 = jnp.bfloat16
F32 = jnp.float32


def _dot(a, b):
    return jnp.dot(a, b, preferred_element_type=F32)


def _rms(x, g):
    return x * lax.rsqrt(jnp.mean(x * x, axis=-1, keepdims=True) + EPS) * g


def _const_spec(shape):
    nd = len(shape)
    return pl.BlockSpec(shape, lambda *_: (0,) * nd, pipeline_mode=pl.Buffered(1))


def _token_tile(n_rows):
    tm = min(TOKEN_TILE, n_rows)
    assert n_rows % tm == 0, (n_rows, tm)
    return tm


def _params(n_axes=1):
    return pltpu.CompilerParams(
        dimension_semantics=("parallel",) * n_axes, vmem_limit_bytes=VMEM_LIMIT)


def _ffn_kernel(h_ref, g_ref, wgu_ref, wd_ref, *rest, final_norm, attn_out):
    rest = list(rest)
    if attn_out:
        a_ref, wv_ref, wo_ref = rest[:3]
        cat_ref = rest.pop()
        rest = rest[3:]
    if final_norm:
        fg_ref, o_ref = rest
    else:
        (o_ref,) = rest
    x = h_ref[...]
    if attn_out:
        for p in range(N_HEADS // 2):
            cat_ref[:, p * LANES:(p + 1) * LANES] = _dot(
                a_ref[:, p * 2 * KV_LORA:(p + 1) * 2 * KV_LORA], wv_ref[p]).astype(BF16)
        x = x + _dot(cat_ref[...], wo_ref[...])
    xn = _rms(x, g_ref[...]).astype(BF16)
    acc = jnp.zeros(x.shape, F32)
    for lo in range(0, D_FF, FF_CHUNK):
        hi = min(lo + FF_CHUNK, D_FF)
        g = _dot(xn, wgu_ref[:, lo:hi])
        u = _dot(xn, wgu_ref[:, D_FF + lo:D_FF + hi])
        a = (g * jax.nn.sigmoid(g) * u).astype(BF16)
        acc = acc + _dot(a, wd_ref[lo:hi, :])
    y = x + 0.5 * acc
    if final_norm:
        y = _rms(y, fg_ref[...])
    o_ref[...] = y


def _layer_spec(shape, layer):
    nd = len(shape) - 1
    return pl.BlockSpec((None,) + tuple(shape[1:]), lambda *_: (layer,) + (0,) * nd,
                        pipeline_mode=pl.Buffered(1))


def _ffn(h, g, wgu, wd, layer, final_g=None, attn=None):
    t = h.shape[0]
    tm = _token_tile(t)
    if attn is not None:
        tm = min(tm, FUSED_TOKEN_TILE)
    tok = lambda w: pl.BlockSpec((tm, w), lambda i: (i, 0))
    in_specs = [tok(D_MODEL), _const_spec((1, D_MODEL)), _layer_spec(wgu.shape, layer),
                _layer_spec(wd.shape, layer)]
    args = [h, g, wgu, wd]
    scratch = []
    if attn is not None:
        a, w_v, w_o = attn
        in_specs += [tok(N_HEADS * KV_LORA), _const_spec(w_v.shape), _const_spec(w_o.shape)]
        args += [a, w_v, w_o]
        scratch = [pltpu.VMEM((tm, D_MODEL), BF16)]
    if final_g is not None:
        in_specs.append(_const_spec((1, D_MODEL)))
        args.append(final_g)
    return pl.pallas_call(
        functools.partial(_ffn_kernel, final_norm=final_g is not None,
                          attn_out=attn is not None),
        grid=(t // tm,),
        in_specs=in_specs,
        out_specs=tok(D_MODEL),
        out_shape=jax.ShapeDtypeStruct((t, D_MODEL), F32),
        scratch_shapes=scratch,
        compiler_params=_params(),
        name="ffn",
    )(*args)


def _gmlp_kernel(h_ref, g_ref, win_ref, vg_ref, ws_ref, bias_ref, wout_ref,
                 o_ref, *rest, chunk, emit_v):
    if emit_v:
        v_ref, gate_ref = rest
    else:
        (gate_ref,) = rest
    x = h_ref[...]
    hn = _rms(x, g_ref[...]).astype(BF16)
    z = _dot(hn, win_ref[...])
    z = 0.5 * z * (1.0 + lax.erf(z * math.sqrt(0.5)))
    u = z[:, :D_MODEL]
    v = _rms(z[:, D_MODEL:], vg_ref[...])
    if emit_v:
        v_ref[...] = v
    vb = v.astype(BF16)
    row = lax.broadcasted_iota(jnp.int32, (chunk, chunk), 0)
    col = lax.broadcasted_iota(jnp.int32, (chunk, chunk), 1)
    lower = row >= col
    w = [jnp.where(lower, ws_ref[gi], 0.0).astype(BF16) for gi in range(A_GROUPS)]
    bias = bias_ref[...]
    for c in range(x.shape[0] // chunk):
        rows = slice(c * chunk, (c + 1) * chunk)
        for gi in range(A_GROUPS):
            cols = slice(gi * A_GROUP_DIM, (gi + 1) * A_GROUP_DIM)
            sv = _dot(w[gi], vb[rows, cols]) + bias[:, cols]
            gate_ref[rows, cols] = (u[rows, cols] * sv).astype(BF16)
    o_ref[...] = x + _dot(gate_ref[...], wout_ref[...])


def _gmlp(h, g, w_in, v_g, w_s, bias, w_out, chunk, emit_v):
    t = h.shape[0]
    tm = _token_tile(t)
    assert tm % chunk == 0, (tm, chunk)
    tok = pl.BlockSpec((tm, D_MODEL), lambda i: (i, 0))
    n_out = 2 if emit_v else 1
    return pl.pallas_call(
        functools.partial(_gmlp_kernel, chunk=chunk, emit_v=emit_v),
        grid=(t // tm,),
        in_specs=[tok, _const_spec((1, D_MODEL)), _const_spec(w_in.shape),
                  _const_spec((1, D_MODEL)), _const_spec(w_s.shape),
                  _const_spec(bias.shape), _const_spec(w_out.shape)],
        out_specs=[tok] * n_out,
        out_shape=[jax.ShapeDtypeStruct((t, D_MODEL), F32)] * n_out,
        scratch_shapes=[pltpu.VMEM((tm, D_MODEL), BF16)],
        compiler_params=_params(),
        name="gmlp",
    )(h, g, w_in, v_g, w_s, bias, w_out)


def _latent_kernel(h_ref, g_ref, wc_ref, wr_ref, wrr_ref, cg_ref, cos_ref, sin_ref,
                   ckv_ref, kr_ref, *kv_refs):
    hn = _rms(h_ref[...], g_ref[...]).astype(BF16)
    ckv = _rms(_dot(hn, wc_ref[...]), cg_ref[...])
    kr4 = _dot(hn, wr_ref[...]) * cos_ref[...] + _dot(hn, wrr_ref[...]) * sin_ref[...]
    ckv_ref[...] = ckv
    kr_ref[...] = kr4[:, :QK_ROPE]
    if kv_refs:
        keys_ref, vals_ref = kv_refs
        keys_ref[0, 0, :, :KV_LORA] = ckv.astype(BF16)
        keys_ref[0, 0, :, KV_LORA:] = kr4.astype(BF16)
        vals_ref[0, 0, :KV_LORA, :] = ckv.T.astype(BF16)
        vals_ref[0, 0, KV_LORA:, :] = jnp.ones((V_ROWS - KV_LORA, ATT_TK), BF16)


def _latent(h, g, w_c, w_r4, w_rr4, c_g, cos4, sin4, batch, emit_kv):
    t = h.shape[0]
    s = t // batch
    tm = min(ATT_TK, s)
    assert s % tm == 0 and (tm == ATT_TK or not emit_kv)
    n_pos = s // tm
    tok = lambda w: pl.BlockSpec((tm, w), lambda i: (i, 0))
    pos = pl.BlockSpec((tm, LANES), lambda i: (i % n_pos, 0))
    out_specs = [tok(KV_LORA), tok(QK_ROPE)]
    out_shape = [jax.ShapeDtypeStruct((t, KV_LORA), F32),
                 jax.ShapeDtypeStruct((t, QK_ROPE), F32)]
    if emit_kv:
        tile = lambda r, c: pl.BlockSpec((1, 1, r, c), lambda i: (i // n_pos, i % n_pos, 0, 0))
        out_specs += [tile(ATT_TK, QK_WIDTH), tile(V_ROWS, ATT_TK)]
        out_shape += [jax.ShapeDtypeStruct((batch, n_pos, ATT_TK, QK_WIDTH), BF16),
                      jax.ShapeDtypeStruct((batch, n_pos, V_ROWS, ATT_TK), BF16)]
    return pl.pallas_call(
        _latent_kernel,
        grid=(t // tm,),
        in_specs=[tok(D_MODEL), _const_spec((1, D_MODEL)), _const_spec(w_c.shape),
                  _const_spec(w_r4.shape), _const_spec(w_rr4.shape),
                  _const_spec((1, KV_LORA)), pos, pos],
        out_specs=out_specs,
        out_shape=out_shape,
        compiler_params=_params(),
        name="latent",
    )(h, g, w_c, w_r4, w_rr4, c_g, cos4, sin4)


def _qproj_kernel(h_ref, g_ref, wdq_ref, qg_ref, wn_ref, wk_ref, wr_ref, wrr_ref,
                  cos_ref, sin_ref, qt_ref):
    tm = h_ref.shape[0]
    cols = cos_ref.shape[1]
    hn = _rms(h_ref[...], g_ref[...]).astype(BF16)
    qc = _rms(_dot(hn, wdq_ref[...]), qg_ref[...])
    if tm < cols:
        qc = jnp.concatenate([qc, jnp.zeros((cols - tm, Q_LORA), F32)], axis=0)
    qc_t = qc.T.astype(BF16)
    qn_t = _dot(wn_ref[...], qc_t).astype(BF16)
    n_blocks = N_HEADS // HEADS_PER_ROPE_BLOCK
    cos = jnp.concatenate([cos_ref[...]] * n_blocks, axis=0)
    sin = jnp.concatenate([sin_ref[...]] * n_blocks, axis=0)
    qr_t = (_dot(wr_ref[...], qc_t) * cos + _dot(wrr_ref[...], qc_t) * sin) * Q_SCALE
    n_tiles = cols // ATT_TQ

    def put(hd, d0, x):
        for i in range(n_tiles):
            qt_ref[0, i, d0:d0 + LANES, hd * ATT_TQ:(hd + 1) * ATT_TQ] = (
                x[:, i * ATT_TQ:(i + 1) * ATT_TQ].astype(BF16))

    for p in range(N_HEADS // 2):
        qa_t = _dot(wk_ref[p], qn_t[p * LANES:(p + 1) * LANES]) * Q_SCALE
        put(2 * p, 0, qa_t[:KV_LORA])
        put(2 * p + 1, 0, qa_t[KV_LORA:])
    row_head = lax.broadcasted_iota(jnp.int32, (LANES, cols), 0) // QK_ROPE
    for hd in range(N_HEADS):
        blk = hd // HEADS_PER_ROPE_BLOCK
        own = row_head == hd % HEADS_PER_ROPE_BLOCK
        put(hd, KV_LORA, jnp.where(own, qr_t[blk * LANES:(blk + 1) * LANES], 0.0))


def _qproj(h, g, w_dq, q_g, w_n, w_k, w_r, w_rr, cos_t, sin_t, batch):
    t = h.shape[0]
    s = t // batch
    tm = _token_tile(s)
    n_s = s // tm
    cols = max(tm, ATT_TQ)
    n_tiles = cols // ATT_TQ
    tok = pl.BlockSpec((tm, D_MODEL), lambda b, i: (b * n_s + i, 0))
    pos = pl.BlockSpec((LANES, cols), lambda b, i: (0, i))
    return pl.pallas_call(
        _qproj_kernel,
        grid=(batch, n_s),
        in_specs=[tok, _const_spec((1, D_MODEL)), _const_spec(w_dq.shape),
                  _const_spec((1, Q_LORA)), _const_spec(w_n.shape), _const_spec(w_k.shape),
                  _const_spec(w_r.shape), _const_spec(w_rr.shape), pos, pos],
        out_specs=pl.BlockSpec((1, n_tiles, QK_WIDTH, N_HEADS * ATT_TQ),
                               lambda b, i: (b, i, 0, 0)),
        out_shape=jax.ShapeDtypeStruct(
            (batch, n_s * n_tiles, QK_WIDTH, N_HEADS * ATT_TQ), BF16),
        compiler_params=_params(2),
        name="qproj",
    )(h, g, w_dq, q_g, w_n, w_k, w_r, w_rr, cos_t, sin_t)


def _attn_kernel(qt_ref, qn_ref, k_ref, vt_ref, o_ref, s0_ref, s1_ref, s2_ref, s3_ref, smax_ref,
                 m_ref, acc_ref, *, causal, n_keys, prefetch):
    n_cols = N_HEADS * ATT_TQ
    step = pl.program_id(1)
    m_ref[...] = jnp.full(m_ref.shape, -jnp.inf, F32)
    acc_ref[...] = jnp.zeros(acc_ref.shape, F32)

    if causal:
        q_start = step * ATT_TQ
        n_full = q_start // ATT_TK
        q_pos = q_start + lax.broadcasted_iota(jnp.int32, (1, n_cols), 1) % ATT_TQ
        limit = (q_pos // CHUNK + 1) * CHUNK
    else:
        n_full = (n_keys - 1) // ATT_TK
        limit = n_keys

    s_refs = (s0_ref, s1_ref, s2_ref, s3_ref)
    LAST_BUF = 3

    def scores(kt, slot, q_ref=qt_ref):
        s = _dot(k_ref[0, kt], q_ref[0, 0])
        s_refs[slot][...] = s
        if slot != LAST_BUF:
            smax_ref[slot:slot + 1, :] = jnp.max(s, axis=0, keepdims=True)

    def softmax_pv(kt, slot, n_rows=None):
        s_ref = s_refs[slot]
        if n_rows is None:
            n_rows = ATT_TK
            smax = smax_ref[slot:slot + 1, :]
        else:
            k_pos = kt * ATT_TK + lax.broadcasted_iota(jnp.int32, (n_rows, 1), 0)
            s = jnp.where(k_pos < limit, s_ref[:n_rows, :], -jnp.inf)
            s_ref[:n_rows, :] = s
            smax = jnp.max(s, axis=0, keepdims=True)
        m_prev = m_ref[...]
        m_new = jnp.maximum(m_prev, smax)
        m_ref[...] = m_new
        p = jnp.exp2(s_ref[:n_rows, :] - m_new)
        alpha = jnp.exp2(m_prev - m_new)
        acc_ref[...] = acc_ref[...] * alpha + _dot(vt_ref[0, kt, :, :n_rows], p.astype(BF16))

    def finish(slot, n_rows):
        if prefetch and slot != 0:
            scores(0, 0, qn_ref)
        softmax_pv(n_full, slot, n_rows)
        if prefetch and slot == 0:
            scores(0, 0, qn_ref)

    def last_tile(slot):
        if causal:
            visible = q_start % ATT_TK + ATT_TQ
            r = (visible - 1) // ATT_LAST_STEP
            for k in range(ATT_TK // ATT_LAST_STEP):
                pl.when(r == k)(functools.partial(finish, slot, (k + 1) * ATT_LAST_STEP))
        else:
            tail = n_keys - n_full * ATT_TK
            finish(slot, -(-tail // LANES) * LANES)

    def triple_body(i, carry):
        kt = 3 * i
        scores(kt + 1, 1)
        scores(kt + 2, 2)
        softmax_pv(kt, 0)
        scores(kt + 3, 0)
        softmax_pv(kt + 1, 1)
        softmax_pv(kt + 2, 2)
        return carry

    def tail(first, n_left):
        for j in range(1, n_left):
            scores(first + j, j)
        for j in range(n_left):
            if j == min(1, n_left - 1):
                scores(n_full, LAST_BUF)
            softmax_pv(first + j, j)

    if prefetch:
        pl.when(step == 0)(functools.partial(scores, 0, 0))
    else:
        scores(0, 0)
    if causal:
        pl.when(n_full == 0)(functools.partial(finish, 0, ATT_TK))

        @pl.when(n_full > 0)
        def _():
            n_triples = (n_full - 1) // 3
            lax.fori_loop(0, n_triples, triple_body, 0)
            first = 3 * n_triples
            for n_left in (1, 2, 3):
                pl.when(n_full - first == n_left)(functools.partial(tail, first, n_left))
            last_tile(LAST_BUF)
    else:
        assert n_full > 0
        n_triples = (n_full - 1) // 3
        lax.fori_loop(0, n_triples, triple_body, 0)
        tail(3 * n_triples, n_full - 3 * n_triples)
        last_tile(LAST_BUF)

    o = acc_ref[:KV_LORA, :] / acc_ref[KV_LORA:KV_LORA + 1, :]
    rows_out = o_ref.shape[1]
    for hd in range(N_HEADS):
        o_hd = o[:, hd * ATT_TQ:(hd + 1) * ATT_TQ].T
        o_ref[0, :, hd * KV_LORA:(hd + 1) * KV_LORA] = o_hd[:rows_out].astype(BF16)


def _attention(qt, keys, vals_t, s, causal, n_keys):
    batch, n_q = qt.shape[:2]
    n_kt = keys.shape[1]
    assert ATT_TK % ATT_TQ == 0 and (s % ATT_TQ == 0 or s < ATT_TQ)
    assert n_kt * ATT_TK >= (s if causal else n_keys)
    rows_out = min(s, ATT_TQ)
    n_cols = N_HEADS * ATT_TQ
    q_spec = lambda nxt: pl.BlockSpec(
        (1, 1, QK_WIDTH, n_cols), lambda b, i: (b, jnp.minimum(i + nxt, n_q - 1), 0, 0))
    score_buf = pltpu.VMEM((ATT_TK, n_cols), F32)
    return pl.pallas_call(
        functools.partial(_attn_kernel, causal=causal, n_keys=n_keys, prefetch=n_q > 1),
        grid=(batch, n_q),
        in_specs=[q_spec(0), q_spec(1),
                  pl.BlockSpec((1, n_kt, ATT_TK, QK_WIDTH), lambda b, i: (b, 0, 0, 0)),
                  pl.BlockSpec((1, n_kt, V_ROWS, ATT_TK), lambda b, i: (b, 0, 0, 0))],
        out_specs=pl.BlockSpec((1, rows_out, N_HEADS * KV_LORA), lambda b, i: (b, i, 0)),
        out_shape=jax.ShapeDtypeStruct((batch, s, N_HEADS * KV_LORA), BF16),
        scratch_shapes=[score_buf, score_buf, score_buf, score_buf,
                        pltpu.VMEM((3, n_cols), F32), pltpu.VMEM((1, n_cols), F32),
                        pltpu.VMEM((V_ROWS, n_cols), F32)],
        compiler_params=pltpu.CompilerParams(
            dimension_semantics=("arbitrary", "arbitrary"), vmem_limit_bytes=VMEM_LIMIT),
        name="attention",
    )(qt, qt, keys, vals_t)


def _block_diag_pairs(w):
    n2, r, c = w.shape
    w = w.reshape(n2 // 2, 2, r, c)
    z = jnp.zeros_like(w[:, 0])
    top = jnp.concatenate([w[:, 0], z], axis=2)
    bot = jnp.concatenate([z, w[:, 1]], axis=2)
    return jnp.concatenate([top, bot], axis=1)


def _rot_half_cols(w):
    shp = w.shape
    w = w.reshape(shp[0], -1, 2, QK_ROPE // 2)
    return jnp.stack([-w[:, :, 1], w[:, :, 0]], axis=2).reshape(shp)


def _rope_tables(pos):
    half = QK_ROPE // 2
    inv = 1.0 / (ROPE_BASE ** (jnp.arange(half, dtype=F32) * (2.0 / QK_ROPE)))
    ang = pos.astype(F32)[:, None] * inv[None, :]
    cos = jnp.cos(ang)
    sin = jnp.sin(ang)
    return jnp.concatenate([cos, cos], axis=1), jnp.concatenate([sin, sin], axis=1)


def _key_value_tiles(ckv, kr):
    b, k, _ = ckv.shape
    n = -(-k // ATT_TK)
    pad = ((0, 0), (0, n * ATT_TK - k), (0, 0))
    ckv = jnp.pad(ckv.astype(BF16), pad)
    kr = jnp.pad(kr.astype(BF16), pad)
    keys = jnp.concatenate([ckv] + [kr] * HEADS_PER_ROPE_BLOCK, axis=2)
    keys = keys.reshape(b, n, ATT_TK, QK_WIDTH)
    vals_t = ckv.reshape(b, n, ATT_TK, KV_LORA).transpose(0, 1, 3, 2)
    ones = jnp.ones((b, n, V_ROWS - KV_LORA, ATT_TK), BF16)
    return keys, jnp.concatenate([vals_t, ones], axis=2)


def kernel(x_prompt, x_sample, cache_ckv, cache_krope, ffn1_norm, ffn1_w_gu, ffn1_w_down, mix_norm, ffn2_norm, ffn2_w_gu, ffn2_w_down, a_w_in, a_v_norm, a_w_s, a_b_s, a_w_out, kv_norm, w_dkv, ckv_norm, w_uk, w_uv, b_w_dq, b_q_norm, b_w_uq, b_w_o, final_norm):
    n_b = DEPTH - N_A
    row = lambda v: v.reshape(1, -1)

    ffn1 = (ffn1_w_gu.astype(BF16), ffn1_w_down.astype(BF16))
    ffn2 = (ffn2_w_gu.astype(BF16), ffn2_w_down.astype(BF16))
    a_in = a_w_in.astype(BF16)
    a_out = a_w_out.astype(BF16)

    w_c = w_dkv[:, :KV_LORA].astype(BF16)
    w_kr = w_dkv[:, KV_LORA:]
    w_kr4 = jnp.tile(w_kr, (1, HEADS_PER_ROPE_BLOCK)).astype(BF16)
    w_kr_rot4 = jnp.tile(_rot_half_cols(w_kr), (1, HEADS_PER_ROPE_BLOCK)).astype(BF16)

    w_uq = b_w_uq.reshape(n_b, Q_LORA, N_HEADS, QK_NOPE + QK_ROPE)
    w_qn = w_uq[..., :QK_NOPE].reshape(n_b, Q_LORA, N_HEADS * QK_NOPE)
    w_qn_t = w_qn.transpose(0, 2, 1).astype(BF16)
    w_qr = w_uq[..., QK_NOPE:].reshape(n_b, Q_LORA, N_HEADS * QK_ROPE)
    w_qr_rot_t = jnp.stack([_rot_half_cols(w_qr[j]).T for j in range(n_b)]).astype(BF16)
    w_qr_t = w_qr.transpose(0, 2, 1).astype(BF16)
    w_dq = b_w_dq.astype(BF16)
    uk_heads = w_uk.reshape(KV_LORA, N_HEADS, QK_NOPE).transpose(1, 2, 0)
    w_absorb_t = _block_diag_pairs(uk_heads).transpose(0, 2, 1).astype(BF16)
    uv_heads = w_uv.reshape(KV_LORA, N_HEADS, V_DIM).transpose(1, 0, 2)
    w_expand = _block_diag_pairs(uv_heads).astype(BF16)
    w_o = b_w_o.astype(BF16)

    def run(x, pos, chunk, past_ckv, past_krope):
        batch, s, _ = x.shape
        causal = past_ckv is None
        h = x.reshape(batch * s, D_MODEL)
        cos, sin = _rope_tables(pos)
        cos4 = jnp.concatenate([cos] * HEADS_PER_ROPE_BLOCK, axis=1)
        sin4 = jnp.concatenate([sin] * HEADS_PER_ROPE_BLOCK, axis=1)
        lane_pad = ((0, 0), (0, max(ATT_TQ - s, 0)))
        cos_t = jnp.pad(cos4.T, lane_pad)
        sin_t = jnp.pad(sin4.T, lane_pad)
        v_rows = []
        ckv = kr = keys = vals_t = None
        n_keys = s
        for l in range(DEPTH):
            h = _ffn(h, row(ffn1_norm[l]), *ffn1, l)
            attn = None
            if l < N_A:
                bias = jnp.repeat(a_b_s[l][:, :chunk].T, A_GROUP_DIM, axis=1)
                h, *v = _gmlp(h, row(mix_norm[l]), a_in[l], row(a_v_norm[l]),
                              a_w_s[l][:, :chunk, :chunk], bias, a_out[l], chunk,
                              emit_v=not causal)
                v_rows.extend(vi.reshape(batch, s, D_MODEL) for vi in v)
            else:
                j = l - N_A
                qt = _qproj(h, row(mix_norm[l]), w_dq[j], row(b_q_norm[j]), w_qn_t[j],
                            w_absorb_t, w_qr_t[j], w_qr_rot_t[j], cos_t, sin_t, batch)
                a = _attention(qt, keys, vals_t, s, causal, n_keys)
                attn = (a.reshape(batch * s, N_HEADS * KV_LORA), w_expand, w_o[j])
            h = _ffn(h, row(ffn2_norm[l]), *ffn2, l, attn=attn,
                     final_g=row(final_norm) if l == DEPTH - 1 else None)
            if l == N_A - 1:
                ckv, kr, *kv = _latent(h, row(kv_norm), w_c, w_kr4, w_kr_rot4, row(ckv_norm),
                                       cos4, sin4, batch, emit_kv=causal)
                ckv = ckv.reshape(batch, s, KV_LORA)
                kr = kr.reshape(batch, s, QK_ROPE)
                if causal:
                    keys, vals_t = kv
                else:
                    ckv_all = jnp.concatenate([past_ckv, ckv], axis=1)
                    kr_all = jnp.concatenate([past_krope, kr], axis=1)
                    n_keys = ckv_all.shape[1]
                    keys, vals_t = _key_value_tiles(ckv_all, kr_all)
        return h.reshape(batch, s, D_MODEL), ckv, kr, v_rows

    s_p = x_prompt.shape[1]
    y_p, ckv_p, kr_p, _ = run(x_prompt, jnp.arange(s_p), A_CHUNK, None, None)
    past = cache_ckv.shape[1]
    s_s = x_sample.shape[1]
    y_s, ckv_s, kr_s, a_rows = run(x_sample, past + jnp.arange(s_s), s_s, cache_ckv, cache_krope)
    return (y_p, y_s, ckv_p, kr_p, ckv_s, kr_s, jnp.stack(a_rows, axis=0))
```

```python
import functools
import math

import jax
import jax.numpy as jnp
from jax import lax
from jax.experimental import pallas as pl
from jax.experimental.pallas import tpu as pltpu

D_MODEL = 1024
DEPTH = 4
N_A = DEPTH // 2
CHUNK = 64
A_CHUNK = 128
A_GROUPS = 8
A_GROUP_DIM = D_MODEL // A_GROUPS
N_HEADS = 16
QK_NOPE = 64
QK_ROPE = 32
V_DIM = 64
KV_LORA = 128
Q_LORA = 256
D_FF = 2816
ROPE_BASE = 10000.0
EPS = 1e-6
ATT_SCALE = (QK_NOPE + QK_ROPE) ** -0.5
Q_SCALE = ATT_SCALE * math.log2(math.e)

LANES = 128
MXU_DIM = 256
VMEM_LIMIT = 56 * 1024 * 1024

TOKEN_TILE = 1024
FUSED_TOKEN_TILE = 512
FF_CHUNK = MXU_DIM
HEADS_PER_ROPE_BLOCK = LANES // QK_ROPE
QK_WIDTH = KV_LORA + LANES
ATT_TQ = LANES
ATT_TK = 2 * MXU_DIM
ATT_LAST_STEP = LANES
BF16_SUBLANES = 16
V_ROWS = KV_LORA + BF16_SUBLANES

BF16 = jnp.bfloat16
F32 = jnp.float32


def _dot(a, b):
    return jnp.dot(a, b, preferred_element_type=F32)


def _rms(x, g):
    return x * lax.rsqrt(jnp.mean(x * x, axis=-1, keepdims=True) + EPS) * g


def _const_spec(shape):
    nd = len(shape)
    return pl.BlockSpec(shape, lambda *_: (0,) * nd, pipeline_mode=pl.Buffered(1))


def _token_tile(n_rows):
    tm = min(TOKEN_TILE, n_rows)
    assert n_rows % tm == 0, (n_rows, tm)
    return tm


def _params(n_axes=1):
    return pltpu.CompilerParams(
        dimension_semantics=("parallel",) * n_axes, vmem_limit_bytes=VMEM_LIMIT)


def _ffn_kernel(h_ref, g_ref, wgu_ref, wd_ref, *rest, final_norm, attn_out):
    rest = list(rest)
    if attn_out:
        a_ref, wv_ref, wo_ref = rest[:3]
        cat_ref = rest.pop()
        rest = rest[3:]
    if final_norm:
        fg_ref, o_ref = rest
    else:
        (o_ref,) = rest
    x = h_ref[...]
    if attn_out:
        for p in range(N_HEADS // 2):
            cat_ref[:, p * LANES:(p + 1) * LANES] = _dot(
                a_ref[:, p * 2 * KV_LORA:(p + 1) * 2 * KV_LORA], wv_ref[p]).astype(BF16)
        x = x + _dot(cat_ref[...], wo_ref[...])
    xn = _rms(x, g_ref[...]).astype(BF16)
    acc = jnp.zeros(x.shape, F32)
    for lo in range(0, D_FF, FF_CHUNK):
        hi = min(lo + FF_CHUNK, D_FF)
        g = _dot(xn, wgu_ref[:, lo:hi])
        u = _dot(xn, wgu_ref[:, D_FF + lo:D_FF + hi])
        a = (g * jax.nn.sigmoid(g) * u).astype(BF16)
        acc = acc + _dot(a, wd_ref[lo:hi, :])
    y = x + 0.5 * acc
    if final_norm:
        y = _rms(y, fg_ref[...])
    o_ref[...] = y


def _layer_spec(shape, layer):
    nd = len(shape) - 1
    return pl.BlockSpec((None,) + tuple(shape[1:]), lambda *_: (layer,) + (0,) * nd,
                        pipeline_mode=pl.Buffered(1))


def _ffn(h, g, wgu, wd, layer, final_g=None, attn=None):
    t = h.shape[0]
    tm = _token_tile(t)
    if attn is not None:
        tm = min(tm, FUSED_TOKEN_TILE)
    tok = lambda w: pl.BlockSpec((tm, w), lambda i: (i, 0))
    in_specs = [tok(D_MODEL), _const_spec((1, D_MODEL)), _layer_spec(wgu.shape, layer),
                _layer_spec(wd.shape, layer)]
    args = [h, g, wgu, wd]
    scratch = []
    if attn is not None:
        a, w_v, w_o = attn
        in_specs += [tok(N_HEADS * KV_LORA), _const_spec(w_v.shape), _const_spec(w_o.shape)]
        args += [a, w_v, w_o]
        scratch = [pltpu.VMEM((tm, D_MODEL), BF16)]
    if final_g is not None:
        in_specs.append(_const_spec((1, D_MODEL)))
        args.append(final_g)
    return pl.pallas_call(
        functools.partial(_ffn_kernel, final_norm=final_g is not None,
                          attn_out=attn is not None),
        grid=(t // tm,),
        in_specs=in_specs,
        out_specs=tok(D_MODEL),
        out_shape=jax.ShapeDtypeStruct((t, D_MODEL), F32),
        scratch_shapes=scratch,
        compiler_params=_params(),
        name="ffn",
    )(*args)


def _gmlp_kernel(h_ref, g_ref, win_ref, vg_ref, ws_ref, bias_ref, wout_ref,
                 o_ref, *rest, chunk, emit_v):
    if emit_v:
        v_ref, gate_ref = rest
    else:
        (gate_ref,) = rest
    x = h_ref[...]
    hn = _rms(x, g_ref[...]).astype(BF16)
    z = _dot(hn, win_ref[...])
    z = 0.5 * z * (1.0 + lax.erf(z * math.sqrt(0.5)))
    u = z[:, :D_MODEL]
    v = _rms(z[:, D_MODEL:], vg_ref[...])
    if emit_v:
        v_ref[...] = v
    vb = v.astype(BF16)
    row = lax.broadcasted_iota(jnp.int32, (chunk, chunk), 0)
    col = lax.broadcasted_iota(jnp.int32, (chunk, chunk), 1)
    lower = row >= col
    w = [jnp.where(lower, ws_ref[gi], 0.0).astype(BF16) for gi in range(A_GROUPS)]
    bias = bias_ref[...]
    for c in range(x.shape[0] // chunk):
        rows = slice(c * chunk, (c + 1) * chunk)
        for gi in range(A_GROUPS):
            cols = slice(gi * A_GROUP_DIM, (gi + 1) * A_GROUP_DIM)
            sv = _dot(w[gi], vb[rows, cols]) + bias[:, cols]
            gate_ref[rows, cols] = (u[rows, cols] * sv).astype(BF16)
    o_ref[...] = x + _dot(gate_ref[...], wout_ref[...])


def _gmlp(h, g, w_in, v_g, w_s, bias, w_out, chunk, emit_v):
    t = h.shape[0]
    tm = _token_tile(t)
    assert tm % chunk == 0, (tm, chunk)
    tok = pl.BlockSpec((tm, D_MODEL), lambda i: (i, 0))
    n_out = 2 if emit_v else 1
    return pl.pallas_call(
        functools.partial(_gmlp_kernel, chunk=chunk, emit_v=emit_v),
        grid=(t // tm,),
        in_specs=[tok, _const_spec((1, D_MODEL)), _const_spec(w_in.shape),
                  _const_spec((1, D_MODEL)), _const_spec(w_s.shape),
                  _const_spec(bias.shape), _const_spec(w_out.shape)],
        out_specs=[tok] * n_out,
        out_shape=[jax.ShapeDtypeStruct((t, D_MODEL), F32)] * n_out,
        scratch_shapes=[pltpu.VMEM((tm, D_MODEL), BF16)],
        compiler_params=_params(),
        name="gmlp",
    )(h, g, w_in, v_g, w_s, bias, w_out)


def _latent_kernel(h_ref, g_ref, wc_ref, wr_ref, wrr_ref, cg_ref, cos_ref, sin_ref,
                   ckv_ref, kr_ref, *kv_refs):
    hn = _rms(h_ref[...], g_ref[...]).astype(BF16)
    ckv = _rms(_dot(hn, wc_ref[...]), cg_ref[...])
    kr4 = _dot(hn, wr_ref[...]) * cos_ref[...] + _dot(hn, wrr_ref[...]) * sin_ref[...]
    ckv_ref[...] = ckv
    kr_ref[...] = kr4[:, :QK_ROPE]
    if kv_refs:
        keys_ref, vals_ref = kv_refs
        for j in range(keys_ref.shape[1]):
            rows = slice(j * ATT_TK, (j + 1) * ATT_TK)
            keys_ref[0, j, :, :KV_LORA] = ckv[rows].astype(BF16)
            keys_ref[0, j, :, KV_LORA:] = kr4[rows].astype(BF16)
            vals_ref[0, j, :KV_LORA, :] = ckv[rows].T.astype(BF16)
            vals_ref[0, j, KV_LORA:, :] = jnp.ones((V_ROWS - KV_LORA, ATT_TK), BF16)


def _latent(h, g, w_c, w_r4, w_rr4, c_g, cos4, sin4, batch, emit_kv):
    t = h.shape[0]
    s = t // batch
    tm = _token_tile(s)
    assert tm % ATT_TK == 0 or not emit_kv
    tiles = tm // ATT_TK
    n_pos = s // tm
    tok = lambda w: pl.BlockSpec((tm, w), lambda i: (i, 0))
    pos = pl.BlockSpec((tm, LANES), lambda i: (i % n_pos, 0))
    out_specs = [tok(KV_LORA), tok(QK_ROPE)]
    out_shape = [jax.ShapeDtypeStruct((t, KV_LORA), F32),
                 jax.ShapeDtypeStruct((t, QK_ROPE), F32)]
    if emit_kv:
        tile = lambda r, c: pl.BlockSpec((1, tiles, r, c),
                                         lambda i: (i // n_pos, i % n_pos, 0, 0))
        out_specs += [tile(ATT_TK, QK_WIDTH), tile(V_ROWS, ATT_TK)]
        out_shape += [jax.ShapeDtypeStruct((batch, n_pos * tiles, ATT_TK, QK_WIDTH), BF16),
                      jax.ShapeDtypeStruct((batch, n_pos * tiles, V_ROWS, ATT_TK), BF16)]
    return pl.pallas_call(
        _latent_kernel,
        grid=(t // tm,),
        in_specs=[tok(D_MODEL), _const_spec((1, D_MODEL)), _const_spec(w_c.shape),
                  _const_spec(w_r4.shape), _const_spec(w_rr4.shape),
                  _const_spec((1, KV_LORA)), pos, pos],
        out_specs=out_specs,
        out_shape=out_shape,
        compiler_params=_params(),
        name="latent",
    )(h, g, w_c, w_r4, w_rr4, c_g, cos4, sin4)


def _qproj_kernel(h_ref, g_ref, wdq_ref, qg_ref, wn_ref, wk_ref, wr_ref, wrr_ref,
                  cos_ref, sin_ref, qt_ref):
    tm = h_ref.shape[0]
    cols = cos_ref.shape[1]
    hn = _rms(h_ref[...], g_ref[...]).astype(BF16)
    qc = _rms(_dot(hn, wdq_ref[...]), qg_ref[...])
    if tm < cols:
        qc = jnp.concatenate([qc, jnp.zeros((cols - tm, Q_LORA), F32)], axis=0)
    qc_t = qc.T.astype(BF16)
    qn_t = _dot(wn_ref[...], qc_t).astype(BF16)
    n_blocks = N_HEADS // HEADS_PER_ROPE_BLOCK
    cos = jnp.concatenate([cos_ref[...]] * n_blocks, axis=0)
    sin = jnp.concatenate([sin_ref[...]] * n_blocks, axis=0)
    qr_t = (_dot(wr_ref[...], qc_t) * cos + _dot(wrr_ref[...], qc_t) * sin) * Q_SCALE
    n_tiles = cols // ATT_TQ

    def put(hd, d0, x):
        for i in range(n_tiles):
            qt_ref[0, i, d0:d0 + LANES, hd * ATT_TQ:(hd + 1) * ATT_TQ] = (
                x[:, i * ATT_TQ:(i + 1) * ATT_TQ].astype(BF16))

    for p in range(N_HEADS // 2):
        qa_t = _dot(wk_ref[p], qn_t[p * LANES:(p + 1) * LANES]) * Q_SCALE
        put(2 * p, 0, qa_t[:KV_LORA])
        put(2 * p + 1, 0, qa_t[KV_LORA:])
    row_head = lax.broadcasted_iota(jnp.int32, (LANES, cols), 0) // QK_ROPE
    for hd in range(N_HEADS):
        blk = hd // HEADS_PER_ROPE_BLOCK
        own = row_head == hd % HEADS_PER_ROPE_BLOCK
        put(hd, KV_LORA, jnp.where(own, qr_t[blk * LANES:(blk + 1) * LANES], 0.0))


def _qproj(h, g, w_dq, q_g, w_n, w_k, w_r, w_rr, cos_t, sin_t, batch):
    t = h.shape[0]
    s = t // batch
    tm = _token_tile(s)
    n_s = s // tm
    cols = max(tm, ATT_TQ)
    n_tiles = cols // ATT_TQ
    tok = pl.BlockSpec((tm, D_MODEL), lambda b, i: (b * n_s + i, 0))
    pos = pl.BlockSpec((LANES, cols), lambda b, i: (0, i))
    return pl.pallas_call(
        _qproj_kernel,
        grid=(batch, n_s),
        in_specs=[tok, _const_spec((1, D_MODEL)), _const_spec(w_dq.shape),
                  _const_spec((1, Q_LORA)), _const_spec(w_n.shape), _const_spec(w_k.shape),
                  _const_spec(w_r.shape), _const_spec(w_rr.shape), pos, pos],
        out_specs=pl.BlockSpec((1, n_tiles, QK_WIDTH, N_HEADS * ATT_TQ),
                               lambda b, i: (b, i, 0, 0)),
        out_shape=jax.ShapeDtypeStruct(
            (batch, n_s * n_tiles, QK_WIDTH, N_HEADS * ATT_TQ), BF16),
        compiler_params=_params(2),
        name="qproj",
    )(h, g, w_dq, q_g, w_n, w_k, w_r, w_rr, cos_t, sin_t)


def _attn_kernel(qt_ref, qn_ref, k_ref, vt_ref, o_ref, s0_ref, s1_ref, s2_ref, s3_ref, smax_ref,
                 m_ref, acc_ref, *, causal, n_keys, prefetch):
    n_cols = N_HEADS * ATT_TQ
    step = pl.program_id(1)
    m_ref[...] = jnp.full(m_ref.shape, -jnp.inf, F32)
    acc_ref[...] = jnp.zeros(acc_ref.shape, F32)

    if causal:
        q_start = step * ATT_TQ
        n_full = q_start // ATT_TK
        q_pos = q_start + lax.broadcasted_iota(jnp.int32, (1, n_cols), 1) % ATT_TQ
        limit = (q_pos // CHUNK + 1) * CHUNK
    else:
        n_full = (n_keys - 1) // ATT_TK
        limit = n_keys

    s_refs = (s0_ref, s1_ref, s2_ref, s3_ref)
    LAST_BUF = 3

    def scores(kt, slot, q_ref=qt_ref):
        s = _dot(k_ref[0, kt], q_ref[0, 0])
        s_refs[slot][...] = s
        if slot != LAST_BUF:
            smax_ref[slot:slot + 1, :] = jnp.max(s, axis=0, keepdims=True)

    def softmax_pv(kt, slot, n_rows=None):
        s_ref = s_refs[slot]
        if n_rows is None:
            n_rows = ATT_TK
            smax = smax_ref[slot:slot + 1, :]
        else:
            k_pos = kt * ATT_TK + lax.broadcasted_iota(jnp.int32, (n_rows, 1), 0)
            s = jnp.where(k_pos < limit, s_ref[:n_rows, :], -jnp.inf)
            s_ref[:n_rows, :] = s
            smax = jnp.max(s, axis=0, keepdims=True)
        m_prev = m_ref[...]
        m_new = jnp.maximum(m_prev, smax)
        m_ref[...] = m_new
        p = jnp.exp2(s_ref[:n_rows, :] - m_new)
        alpha = jnp.exp2(m_prev - m_new)
        acc_ref[...] = acc_ref[...] * alpha + _dot(vt_ref[0, kt, :, :n_rows], p.astype(BF16))

    def finish(slot, n_rows):
        if prefetch and slot != 0:
            scores(0, 0, qn_ref)
        softmax_pv(n_full, slot, n_rows)
        if prefetch and slot == 0:
            scores(0, 0, qn_ref)

    def last_tile(slot):
        if causal:
            visible = q_start % ATT_TK + ATT_TQ
            r = (visible - 1) // ATT_LAST_STEP
            for k in range(ATT_TK // ATT_LAST_STEP):
                pl.when(r == k)(functools.partial(finish, slot, (k + 1) * ATT_LAST_STEP))
        else:
            tail = n_keys - n_full * ATT_TK
            finish(slot, -(-tail // LANES) * LANES)

    def triple_body(i, carry):
        kt = 3 * i
        scores(kt + 1, 1)
        scores(kt + 2, 2)
        softmax_pv(kt, 0)
        scores(kt + 3, 0)
        softmax_pv(kt + 1, 1)
        softmax_pv(kt + 2, 2)
        return carry

    def tail(first, n_left):
        for j in range(1, n_left):
            scores(first + j, j)
        for j in range(n_left):
            if j == min(1, n_left - 1):
                scores(n_full, LAST_BUF)
            softmax_pv(first + j, j)

    if prefetch:
        pl.when(step == 0)(functools.partial(scores, 0, 0))
    else:
        scores(0, 0)
    if causal:
        pl.when(n_full == 0)(functools.partial(finish, 0, ATT_TK))

        @pl.when(n_full > 0)
        def _():
            n_triples = (n_full - 1) // 3
            lax.fori_loop(0, n_triples, triple_body, 0)
            first = 3 * n_triples
            for n_left in (1, 2, 3):
                pl.when(n_full - first == n_left)(functools.partial(tail, first, n_left))
            last_tile(LAST_BUF)
    else:
        assert n_full > 0
        n_triples = (n_full - 1) // 3
        lax.fori_loop(0, n_triples, triple_body, 0)
        tail(3 * n_triples, n_full - 3 * n_triples)
        last_tile(LAST_BUF)

    o = acc_ref[:KV_LORA, :] / acc_ref[KV_LORA:KV_LORA + 1, :]
    rows_out = o_ref.shape[1]
    for hd in range(N_HEADS):
        o_hd = o[:, hd * ATT_TQ:(hd + 1) * ATT_TQ].T
        o_ref[0, :, hd * KV_LORA:(hd + 1) * KV_LORA] = o_hd[:rows_out].astype(BF16)


def _attention(qt, keys, vals_t, s, causal, n_keys):
    batch, n_q = qt.shape[:2]
    n_kt = keys.shape[1]
    assert ATT_TK % ATT_TQ == 0 and (s % ATT_TQ == 0 or s < ATT_TQ)
    assert n_kt * ATT_TK >= (s if causal else n_keys)
    rows_out = min(s, ATT_TQ)
    n_cols = N_HEADS * ATT_TQ
    q_spec = lambda nxt: pl.BlockSpec(
        (1, 1, QK_WIDTH, n_cols), lambda b, i: (b, jnp.minimum(i + nxt, n_q - 1), 0, 0))
    score_buf = pltpu.VMEM((ATT_TK, n_cols), F32)
    return pl.pallas_call(
        functools.partial(_attn_kernel, causal=causal, n_keys=n_keys, prefetch=n_q > 1),
        grid=(batch, n_q),
        in_specs=[q_spec(0), q_spec(1),
                  pl.BlockSpec((1, n_kt, ATT_TK, QK_WIDTH), lambda b, i: (b, 0, 0, 0)),
                  pl.BlockSpec((1, n_kt, V_ROWS, ATT_TK), lambda b, i: (b, 0, 0, 0))],
        out_specs=pl.BlockSpec((1, rows_out, N_HEADS * KV_LORA), lambda b, i: (b, i, 0)),
        out_shape=jax.ShapeDtypeStruct((batch, s, N_HEADS * KV_LORA), BF16),
        scratch_shapes=[score_buf, score_buf, score_buf, score_buf,
                        pltpu.VMEM((3, n_cols), F32), pltpu.VMEM((1, n_cols), F32),
                        pltpu.VMEM((V_ROWS, n_cols), F32)],
        compiler_params=pltpu.CompilerParams(
            dimension_semantics=("arbitrary", "arbitrary"), vmem_limit_bytes=VMEM_LIMIT),
        name="attention",
    )(qt, qt, keys, vals_t)


def _block_diag_pairs(w):
    n2, r, c = w.shape
    w = w.reshape(n2 // 2, 2, r, c)
    z = jnp.zeros_like(w[:, 0])
    top = jnp.concatenate([w[:, 0], z], axis=2)
    bot = jnp.concatenate([z, w[:, 1]], axis=2)
    return jnp.concatenate([top, bot], axis=1)


def _rot_half_cols(w):
    shp = w.shape
    w = w.reshape(shp[0], -1, 2, QK_ROPE // 2)
    return jnp.stack([-w[:, :, 1], w[:, :, 0]], axis=2).reshape(shp)


def _rope_tables(pos):
    half = QK_ROPE // 2
    inv = 1.0 / (ROPE_BASE ** (jnp.arange(half, dtype=F32) * (2.0 / QK_ROPE)))
    ang = pos.astype(F32)[:, None] * inv[None, :]
    cos = jnp.cos(ang)
    sin = jnp.sin(ang)
    return jnp.concatenate([cos, cos], axis=1), jnp.concatenate([sin, sin], axis=1)


def _key_value_tiles(ckv, kr):
    b, k, _ = ckv.shape
    n = -(-k // ATT_TK)
    pad = ((0, 0), (0, n * ATT_TK - k), (0, 0))
    ckv = jnp.pad(ckv.astype(BF16), pad)
    kr = jnp.pad(kr.astype(BF16), pad)
    keys = jnp.concatenate([ckv] + [kr] * HEADS_PER_ROPE_BLOCK, axis=2)
    keys = keys.reshape(b, n, ATT_TK, QK_WIDTH)
    vals_t = ckv.reshape(b, n, ATT_TK, KV_LORA).transpose(0, 1, 3, 2)
    ones = jnp.ones((b, n, V_ROWS - KV_LORA, ATT_TK), BF16)
    return keys, jnp.concatenate([vals_t, ones], axis=2)


def kernel(x_prompt, x_sample, cache_ckv, cache_krope, ffn1_norm, ffn1_w_gu, ffn1_w_down, mix_norm, ffn2_norm, ffn2_w_gu, ffn2_w_down, a_w_in, a_v_norm, a_w_s, a_b_s, a_w_out, kv_norm, w_dkv, ckv_norm, w_uk, w_uv, b_w_dq, b_q_norm, b_w_uq, b_w_o, final_norm):
    n_b = DEPTH - N_A
    row = lambda v: v.reshape(1, -1)

    ffn1 = (ffn1_w_gu.astype(BF16), ffn1_w_down.astype(BF16))
    ffn2 = (ffn2_w_gu.astype(BF16), ffn2_w_down.astype(BF16))
    a_in = a_w_in.astype(BF16)
    a_out = a_w_out.astype(BF16)

    w_c = w_dkv[:, :KV_LORA].astype(BF16)
    w_kr = w_dkv[:, KV_LORA:]
    w_kr4 = jnp.tile(w_kr, (1, HEADS_PER_ROPE_BLOCK)).astype(BF16)
    w_kr_rot4 = jnp.tile(_rot_half_cols(w_kr), (1, HEADS_PER_ROPE_BLOCK)).astype(BF16)

    w_uq = b_w_uq.reshape(n_b, Q_LORA, N_HEADS, QK_NOPE + QK_ROPE)
    w_qn = w_uq[..., :QK_NOPE].reshape(n_b, Q_LORA, N_HEADS * QK_NOPE)
    w_qn_t = w_qn.transpose(0, 2, 1).astype(BF16)
    w_qr = w_uq[..., QK_NOPE:].reshape(n_b, Q_LORA, N_HEADS * QK_ROPE)
    w_qr_rot_t = jnp.stack([_rot_half_cols(w_qr[j]).T for j in range(n_b)]).astype(BF16)
    w_qr_t = w_qr.transpose(0, 2, 1).astype(BF16)
    w_dq = b_w_dq.astype(BF16)
    uk_heads = w_uk.reshape(KV_LORA, N_HEADS, QK_NOPE).transpose(1, 2, 0)
    w_absorb_t = _block_diag_pairs(uk_heads).transpose(0, 2, 1).astype(BF16)
    uv_heads = w_uv.reshape(KV_LORA, N_HEADS, V_DIM).transpose(1, 0, 2)
    w_expand = _block_diag_pairs(uv_heads).astype(BF16)
    w_o = b_w_o.astype(BF16)

    def run(x, pos, chunk, past_ckv, past_krope):
        batch, s, _ = x.shape
        causal = past_ckv is None
        h = x.reshape(batch * s, D_MODEL)
        cos, sin = _rope_tables(pos)
        cos4 = jnp.concatenate([cos] * HEADS_PER_ROPE_BLOCK, axis=1)
        sin4 = jnp.concatenate([sin] * HEADS_PER_ROPE_BLOCK, axis=1)
        lane_pad = ((0, 0), (0, max(ATT_TQ - s, 0)))
        cos_t = jnp.pad(cos4.T, lane_pad)
        sin_t = jnp.pad(sin4.T, lane_pad)
        v_rows = []
        ckv = kr = keys = vals_t = None
        n_keys = s
        for l in range(DEPTH):
            h = _ffn(h, row(ffn1_norm[l]), *ffn1, l)
            attn = None
            if l < N_A:
                bias = jnp.repeat(a_b_s[l][:, :chunk].T, A_GROUP_DIM, axis=1)
                h, *v = _gmlp(h, row(mix_norm[l]), a_in[l], row(a_v_norm[l]),
                              a_w_s[l][:, :chunk, :chunk], bias, a_out[l], chunk,
                              emit_v=not causal)
                v_rows.extend(vi.reshape(batch, s, D_MODEL) for vi in v)
            else:
                j = l - N_A
                qt = _qproj(h, row(mix_norm[l]), w_dq[j], row(b_q_norm[j]), w_qn_t[j],
                            w_absorb_t, w_qr_t[j], w_qr_rot_t[j], cos_t, sin_t, batch)
                a = _attention(qt, keys, vals_t, s, causal, n_keys)
                attn = (a.reshape(batch * s, N_HEADS * KV_LORA), w_expand, w_o[j])
            h = _ffn(h, row(ffn2_norm[l]), *ffn2, l, attn=attn,
                     final_g=row(final_norm) if l == DEPTH - 1 else None)
            if l == N_A - 1:
                ckv, kr, *kv = _latent(h, row(kv_norm), w_c, w_kr4, w_kr_rot4, row(ckv_norm),
                                       cos4, sin4, batch, emit_kv=causal)
                ckv = ckv.reshape(batch, s, KV_LORA)
                kr = kr.reshape(batch, s, QK_ROPE)
                if causal:
                    keys, vals_t = kv
                else:
                    ckv_all = jnp.concatenate([past_ckv, ckv], axis=1)
                    kr_all = jnp.concatenate([past_krope, kr], axis=1)
                    n_keys = ckv_all.shape[1]
                    keys, vals_t = _key_value_tiles(ckv_all, kr_all)
        return h.reshape(batch, s, D_MODEL), ckv, kr, v_rows

    s_p = x_prompt.shape[1]
    y_p, ckv_p, kr_p, _ = run(x_prompt, jnp.arange(s_p), A_CHUNK, None, None)
    past = cache_ckv.shape[1]
    s_s = x_sample.shape[1]
    y_s, ckv_s, kr_s, a_rows = run(x_sample, past + jnp.arange(s_s), s_s, cache_ckv, cache_krope)
    return (y_p, y_s, ckv_p, kr_p, ckv_s, kr_s, jnp.stack(a_rows, axis=0))
```
